```python
import jax, jax.numpy as jnp
from jax import lax
import numpy as np

D_MODEL = 2048
BATCH = 4
SEQ = 2048
DEPTH = 4
DEC_BATCH = 8
DEC_SEQ = 8
PAST_LEN = 16384
PAGE_SIZE = 128

N_MIXERS = 2
N_RWKV = (DEPTH + 1) // 2
N_FOX = DEPTH // 2
RWKV_HEAD_DIM = 64
RWKV_HEADS = D_MODEL // RWKV_HEAD_DIM
LORA_DECAY = 96
LORA_AAA = 96
LORA_MV = 64
LORA_GATE = 256
FOX_HEAD_DIM = 128
FOX_HEADS = D_MODEL // FOX_HEAD_DIM
FORGET_BIAS = 3.0
D_FF = 4 * D_MODEL
D_PLE = 256
Q_BLOCK = 128
RMS_EPS = 1e-6
GN_EPS = 64e-5

kernel_name = 'rwkv7_fox_hybrid_step'


def rms_norm(x, g):
    xf = x.astype(jnp.float32)
    y = xf * lax.rsqrt(jnp.mean(xf * xf, axis=-1, keepdims=True) + RMS_EPS)
    return (y * g.astype(jnp.float32)).astype(x.dtype)


def rwkv7_time_mix(xn, shift_prev, wkv_prev, v_first, j, W):
    f32 = jnp.float32
    B, L, D = xn.shape
    H, N = RWKV_HEADS, RWKV_HEAD_DIM
    x_prev = jnp.concatenate([shift_prev[:, None, :].astype(xn.dtype), xn[:, :-1]], axis=1)
    xx = x_prev - xn
    mu = W['rwkv_mu'][j]
    xr, xw, xk, xv, xa, xg = [xn + xx * mu[c] for c in range(6)]
    r = xr @ W['rwkv_w_r'][j]
    k = xk @ W['rwkv_w_k'][j]
    v = xv @ W['rwkv_w_v'][j]
    w_pre = (W['rwkv_w0'][j] + jnp.tanh(xw @ W['rwkv_w1'][j]) @ W['rwkv_w2'][j]).astype(f32)
    decay = jnp.exp(-jnp.exp(-jax.nn.softplus(-w_pre) - 0.5))
    if j == 0:
        v_first = v
    else:
        v = v + (v_first - v) * jax.nn.sigmoid(W['rwkv_v0'][j - 1] + (xv @ W['rwkv_v1'][j - 1]) @ W['rwkv_v2'][j - 1])
    a = jax.nn.sigmoid(W['rwkv_a0'][j] + (xa @ W['rwkv_a1'][j]) @ W['rwkv_a2'][j])
    g = jax.nn.sigmoid(xg @ W['rwkv_g1'][j]) @ W['rwkv_g2'][j]
    heads = lambda t: t.astype(f32).reshape(B, L, H, N)
    kk = heads(k * W['rwkv_k_k'][j])
    kk = kk * lax.rsqrt(jnp.maximum(jnp.sum(kk * kk, axis=-1, keepdims=True), 1e-24))
    k = k * (1.0 + (a - 1.0) * W['rwkv_k_a'][j])
    rh, kh, vh, ah, wh = heads(r), heads(k), heads(v), heads(a), heads(decay)

    def step(S, inp):
        r_t, w_t, k_t, v_t, kk_t, a_t = inp
        s_kk = jnp.einsum('bhvk,bhk->bhv', S, kk_t)
        S = (S * w_t[:, :, None, :]
             - s_kk[..., None] * (kk_t * a_t)[:, :, None, :]
             + v_t[..., None] * k_t[:, :, None, :])
        return S, jnp.einsum('bhvk,bhk->bhv', S, r_t)

    time_major = lambda t: jnp.swapaxes(t, 0, 1)
    S_last, o = lax.scan(step, wkv_prev.astype(f32),
                         tuple(time_major(t) for t in (rh, wh, kh, vh, kk, ah)))
    o = time_major(o)
    mean = jnp.mean(o, axis=-1, keepdims=True)
    var = jnp.mean(jnp.square(o - mean), axis=-1, keepdims=True)
    o = ((o - mean) * lax.rsqrt(var + GN_EPS)).reshape(B, L, D)
    o = o * W['rwkv_gn_w'][j].astype(f32) + W['rwkv_gn_b'][j].astype(f32)
    bonus = jnp.sum(rh * kh * W['rwkv_r_k'][j].astype(f32), axis=-1, keepdims=True) * vh
    o = (o + bonus.reshape(B, L, D)).astype(xn.dtype) * g
    return o @ W['rwkv_w_o'][j], xn[:, -1], S_last, v_first


def fox_logits(q, k, bq, bk, q_pos, k_pos):
    s = jnp.einsum('bqhd,bkhd->bhqk', q, k, preferred_element_type=jnp.float32) * (FOX_HEAD_DIM ** -0.5)
    s = s + jnp.swapaxes(bq, 1, 2)[:, :, :, None] - jnp.swapaxes(bk, 1, 2)[:, :, None, :]
    return jnp.where(k_pos[None, :] <= q_pos[:, None], s, -jnp.inf)


def fox_attend(q, bq, q_pos, segments):
    logits = jnp.concatenate([fox_logits(q, k, bq, bk, q_pos, kp) for k, _, bk, kp in segments], axis=-1)
    p = jax.nn.softmax(logits, axis=-1)
    outs, off = [], 0
    for k, v, _, _ in segments:
        n = k.shape[1]
        outs.append(jnp.einsum('bhqk,bkhd->bqhd', p[..., off:off + n].astype(v.dtype), v,
                               preferred_element_type=jnp.float32))
        off += n
    return sum(outs[1:], outs[0]).astype(q.dtype)


def fox_mix(xn, j, W, past):
    f32 = jnp.float32
    B, L, D = xn.shape
    H, Dh = FOX_HEADS, FOX_HEAD_DIM
    proj = xn @ W['fox_w_in'][j]
    q = proj[..., :D].reshape(B, L, H, Dh)
    k = proj[..., D:2 * D].reshape(B, L, H, Dh)
    v = proj[..., 2 * D:3 * D].reshape(B, L, H, Dh)
    logf = jax.nn.log_sigmoid(proj[..., 3 * D:].astype(f32) + W['fox_b_f'][j].astype(f32))
    c = jnp.cumsum(logf, axis=1)
    if past is None:
        off, past_segs = 0, []
    else:
        k_past, v_past, logf_past = past
        off = k_past.shape[1]
        suffix = lax.cumsum(logf_past, axis=1, reverse=True) - logf_past
        past_segs = [(k_past, v_past, -suffix, jnp.arange(off))]
    pos = off + jnp.arange(L)
    blocks = []
    for s0 in range(0, L, Q_BLOCK):
        s1 = min(s0 + Q_BLOCK, L)
        segs = past_segs + [(k[:, :s1], v[:, :s1], c[:, :s1], pos[:s1])]
        blocks.append(fox_attend(q[:, s0:s1], c[:, s0:s1], pos[s0:s1], segs))
    o = jnp.concatenate(blocks, axis=1).reshape(B, L, D)
    return o @ W['fox_w_o'][j], k, v, logf


def gather_pages(pool, page_table, j):
    rows = pool[page_table, j]
    return rows.reshape((rows.shape[0], rows.shape[1] * rows.shape[2]) + rows.shape[3:])


def trunk(x, p, shift_in, wkv_in, paged, W):
    v_first = None
    shifts, wkvs, ks, vs, lfs = [], [], [], [], []
    for i in range(DEPTH):
        j = i // N_MIXERS
        h = rms_norm(x, W['norm_mix'][i])
        if i % N_MIXERS == 0:
            o, sh, S, v_first = rwkv7_time_mix(h, shift_in[j], wkv_in[j], v_first, j, W)
            shifts.append(sh)
            wkvs.append(S)
        else:
            past = None
            if paged is not None:
                cache_k, cache_v, cache_logf, page_table = paged
                past = (gather_pages(cache_k, page_table, j),
                        gather_pages(cache_v, page_table, j),
                        gather_pages(cache_logf, page_table, j).astype(jnp.float32))
            o, k, v, lf = fox_mix(h, j, W, past)
            ks.append(k)
            vs.append(v)
            lfs.append(lf)
        x = x + o
        h = rms_norm(x, W['norm_ffn'][i])
        x = x + jnp.square(jax.nn.relu(h @ W['w_up'][i])) @ W['w_down'][i]
        gate = jax.nn.sigmoid(rms_norm(x, W['norm_ple'][i]) @ W['w_ple_gate'][i])
        x = x + gate * (p[i] @ W['w_ple'][i])
    y = rms_norm(x, W['norm_out'])
    return (y, jnp.stack(shifts, 0), jnp.stack(wkvs, 0),
            jnp.stack(ks, 1), jnp.stack(vs, 1), jnp.stack(lfs, 1))


def setup_inputs(seed: int = 0) -> dict:
    key = jax.random.key(seed)
    it = iter(jax.random.split(key, 48))

    def nrm(shape, scale=1.0):
        return jax.random.normal(next(it), shape, jnp.float32) * scale

    D, H, N = D_MODEL, RWKV_HEADS, RWKV_HEAD_DIM
    n_pages = PAST_LEN // PAGE_SIZE
    n_used = DEC_BATCH * n_pages
    n_phys = n_used + (n_used + 3) // 4
    x_prompt = nrm((BATCH, SEQ, D))
    x_sample = nrm((DEC_BATCH, DEC_SEQ, D))
    cache_k = nrm((n_phys, N_FOX, PAGE_SIZE, FOX_HEADS, FOX_HEAD_DIM))
    cache_v = nrm((n_phys, N_FOX, PAGE_SIZE, FOX_HEADS, FOX_HEAD_DIM))
    cache_logf = jax.nn.log_sigmoid(FORGET_BIAS + nrm((n_phys, N_FOX, PAGE_SIZE, FOX_HEADS)))
    state_wkv = nrm((N_RWKV, DEC_BATCH, H, N, N), 0.3)
    state_shift = nrm((N_RWKV, DEC_BATCH, D))
    page_table = jax.random.permutation(next(it), n_phys)[:n_used].reshape(DEC_BATCH, n_pages).astype(jnp.int32)
    p_prompt = nrm((DEPTH, BATCH, SEQ, D_PLE))
    p_sample = nrm((DEPTH, DEC_BATCH, DEC_SEQ, D_PLE))
    gain = lambda shape: 1.0 + nrm(shape, 0.05)
    return {
        'x_prompt': x_prompt, 'x_sample': x_sample,
        'cache_k': cache_k, 'cache_v': cache_v, 'cache_logf': cache_logf,
        'state_wkv': state_wkv, 'state_shift': state_shift, 'page_table': page_table,
        'p_prompt': p_prompt, 'p_sample': p_sample,
        'norm_mix': gain((DEPTH, D)), 'norm_ffn': gain((DEPTH, D)),
        'norm_ple': gain((DEPTH, D)), 'norm_out': gain((D,)),
        'rwkv_mu': jax.random.uniform(next(it), (N_RWKV, 6, D), jnp.float32),
        'rwkv_w_r': nrm((N_RWKV, D, D), D ** -0.5),
        'rwkv_w_k': nrm((N_RWKV, D, D), D ** -0.5),
        'rwkv_w_v': nrm((N_RWKV, D, D), D ** -0.5),
        'rwkv_w_o': nrm((N_RWKV, D, D), D ** -0.5),
        'rwkv_w0': jax.random.uniform(next(it), (N_RWKV, D), jnp.float32, -6.0, -0.5),
        'rwkv_w1': nrm((N_RWKV, D, LORA_DECAY), D ** -0.5),
        'rwkv_w2': nrm((N_RWKV, LORA_DECAY, D), 0.1 * LORA_DECAY ** -0.5),
        'rwkv_a0': nrm((N_RWKV, D), 0.1),
        'rwkv_a1': nrm((N_RWKV, D, LORA_AAA), D ** -0.5),
        'rwkv_a2': nrm((N_RWKV, LORA_AAA, D), 0.5 * LORA_AAA ** -0.5),
        'rwkv_v0': nrm((N_RWKV - 1, D), 0.1),
        'rwkv_v1': nrm((N_RWKV - 1, D, LORA_MV), D ** -0.5),
        'rwkv_v2': nrm((N_RWKV - 1, LORA_MV, D), 0.5 * LORA_MV ** -0.5),
        'rwkv_g1': nrm((N_RWKV, D, LORA_GATE), D ** -0.5),
        'rwkv_g2': nrm((N_RWKV, LORA_GATE, D), LORA_GATE ** -0.5),
        'rwkv_k_k': 0.85 + nrm((N_RWKV, D), 0.05),
        'rwkv_k_a': gain((N_RWKV, D)),
        'rwkv_r_k': nrm((N_RWKV, H, N), 0.1),
        'rwkv_gn_w': gain((N_RWKV, D)),
        'rwkv_gn_b': nrm((N_RWKV, D), 0.01),
        'fox_w_in': nrm((N_FOX, D, 3 * D + FOX_HEADS), D ** -0.5),
        'fox_b_f': FORGET_BIAS + nrm((N_FOX, FOX_HEADS), 0.5),
        'fox_w_o': nrm((N_FOX, D, D), D ** -0.5),
        'w_up': nrm((DEPTH, D, D_FF), D ** -0.5),
        'w_down': nrm((DEPTH, D_FF, D), D_FF ** -0.5),
        'w_ple': nrm((DEPTH, D_PLE, D), D_PLE ** -0.5),
        'w_ple_gate': nrm((DEPTH, D, D), D ** -0.5),
    }


def reference(x_prompt, x_sample, cache_k, cache_v, cache_logf, state_wkv, state_shift, page_table,
              p_prompt, p_sample, norm_mix, norm_ffn, norm_ple, norm_out,
              rwkv_mu, rwkv_w_r, rwkv_w_k, rwkv_w_v, rwkv_w_o, rwkv_w0, rwkv_w1, rwkv_w2,
              rwkv_a0, rwkv_a1, rwkv_a2, rwkv_v0, rwkv_v1, rwkv_v2, rwkv_g1, rwkv_g2,
              rwkv_k_k, rwkv_k_a, rwkv_r_k, rwkv_gn_w, rwkv_gn_b,
              fox_w_in, fox_b_f, fox_w_o, w_up, w_down, w_ple, w_ple_gate):
    W = {
        'norm_mix': norm_mix, 'norm_ffn': norm_ffn, 'norm_ple': norm_ple, 'norm_out': norm_out,
        'rwkv_mu': rwkv_mu, 'rwkv_w_r': rwkv_w_r, 'rwkv_w_k': rwkv_w_k, 'rwkv_w_v': rwkv_w_v,
        'rwkv_w_o': rwkv_w_o, 'rwkv_w0': rwkv_w0, 'rwkv_w1': rwkv_w1, 'rwkv_w2': rwkv_w2,
        'rwkv_a0': rwkv_a0, 'rwkv_a1': rwkv_a1, 'rwkv_a2': rwkv_a2,
        'rwkv_v0': rwkv_v0, 'rwkv_v1': rwkv_v1, 'rwkv_v2': rwkv_v2,
        'rwkv_g1': rwkv_g1, 'rwkv_g2': rwkv_g2, 'rwkv_k_k': rwkv_k_k, 'rwkv_k_a': rwkv_k_a,
        'rwkv_r_k': rwkv_r_k, 'rwkv_gn_w': rwkv_gn_w, 'rwkv_gn_b': rwkv_gn_b,
        'fox_w_in': fox_w_in, 'fox_b_f': fox_b_f, 'fox_w_o': fox_w_o,
        'w_up': w_up, 'w_down': w_down, 'w_ple': w_ple, 'w_ple_gate': w_ple_gate,
    }
    B = x_prompt.shape[0]
    shift0 = jnp.zeros((N_RWKV, B, D_MODEL), x_prompt.dtype)
    wkv0 = jnp.zeros((N_RWKV, B, RWKV_HEADS, RWKV_HEAD_DIM, RWKV_HEAD_DIM), jnp.float32)
    y_prompt, shift_prompt, wkv_prompt, k_prompt, v_prompt, logf_prompt = trunk(
        x_prompt, p_prompt, shift0, wkv0, None, W)
    y_sample, shift_sample, wkv_sample, k_sample, v_sample, logf_sample = trunk(
        x_sample, p_sample, state_shift, state_wkv, (cache_k, cache_v, cache_logf, page_table), W)
    return (y_prompt, y_sample, k_prompt, v_prompt, logf_prompt, k_sample, v_sample, logf_sample,
            wkv_prompt, shift_prompt, wkv_sample, shift_sample)
```

```python
import functools

import jax
import jax.numpy as jnp
from jax import lax
from jax.experimental import pallas as pl
from jax.experimental.pallas import tpu as pltpu

_RMS_EPS = 1e-6
_GN_EPS = 64e-5
_KK_EPS = 1e-24
_NEG = -1e30
_LANES = 128
_VMEM_LIMIT = 52 * 1024 * 1024
_VMEM_BUDGET = 36 * 1024 * 1024
_F32 = jnp.float32
_BF16 = jnp.bfloat16
_HI = lax.Precision.HIGHEST
_NT = (((1,), (1,)), ((), ()))


def _params(*sem):
    return pltpu.CompilerParams(dimension_semantics=sem, vmem_limit_bytes=_VMEM_LIMIT)


def _divisor_tile(n, target, align):
    t = min(n, target)
    t -= t % align
    while t >= align:
        if n % t == 0:
            return t
        t -= align
    return n


def _sigmoid(x):
    return 1.0 / (1.0 + jnp.exp(-x))


def _softplus(x):
    return jnp.maximum(x, 0.0) + jnp.log(1.0 + jnp.exp(-jnp.abs(x)))


def _rms(x, g):
    return x * lax.rsqrt(jnp.mean(x * x, axis=-1, keepdims=True) + _RMS_EPS) * g


def _rmsnorm_kernel(x_ref, g_ref, o_ref):
    o_ref[...] = _rms(x_ref[...], g_ref[...]).astype(o_ref.dtype)


def _rmsnorm(x, g, out_dtype):
    t, d = x.shape
    tr = _divisor_tile(t, 512, 16)
    return pl.pallas_call(
        _rmsnorm_kernel,
        grid=(t // tr,),
        in_specs=[pl.BlockSpec((tr, d), lambda i: (i, 0)), pl.BlockSpec((1, d), lambda i: (0, 0))],
        out_specs=pl.BlockSpec((tr, d), lambda i: (i, 0)),
        out_shape=jax.ShapeDtypeStruct((t, d), out_dtype),
        compiler_params=_params("arbitrary"),
        name="rmsnorm",
    )(x, g.reshape(1, d))


def _mix_kernel(x_ref, xp_ref, sh_ref, g_ref, mu_ref, *o_refs):
    g = g_ref[...]
    xn = _rms(x_ref[0], g)
    prev = _rms(xp_ref[0, 7:8, :], g)
    prev = jnp.where(pl.program_id(1) == 0, sh_ref[0], prev)
    row = lax.broadcasted_iota(jnp.int32, xn.shape, 0)
    xprev = jnp.where(row == 0, prev, pltpu.roll(xn, 1, axis=0))
    xx = xprev - xn
    for c, o_ref in enumerate(o_refs):
        o_ref[0] = (xn + xx * mu_ref[c:c + 1, :]).astype(o_ref.dtype)


def _rwkv_mix(x, shift_prev, g, mu):
    b, l, d = x.shape
    tr = _divisor_tile(l, 256, 8)
    nb8 = tr // 8
    outs = pl.pallas_call(
        _mix_kernel,
        grid=(b, l // tr),
        in_specs=[
            pl.BlockSpec((1, tr, d), lambda bi, i: (bi, i, 0)),
            pl.BlockSpec((1, 8, d), lambda bi, i: (bi, jnp.maximum(i * nb8 - 1, 0), 0)),
            pl.BlockSpec((1, 1, d), lambda bi, i: (bi, 0, 0)),
            pl.BlockSpec((1, d), lambda bi, i: (0, 0)),
            pl.BlockSpec((6, d), lambda bi, i: (0, 0)),
        ],
        out_specs=[pl.BlockSpec((1, tr, d), lambda bi, i: (bi, i, 0))] * 6,
        out_shape=[jax.ShapeDtypeStruct((b, l, d), _BF16)] * 6,
        compiler_params=_params("arbitrary", "arbitrary"),
        name="rwkv_mix",
    )(x, x, shift_prev.reshape(b, 1, d), g.reshape(1, d), mu)
    return [o.reshape(b * l, d) for o in outs]


def _mm_kernel(*refs, x_kinds, n_w, n_emn, n_en, epilogue):
    pos = 0
    xs = []
    for kind in x_kinds:
        if kind == 1:
            xs.append(refs[pos][...])
        else:
            xs.append((refs[pos][...] * refs[pos + 1][...]).astype(_BF16))
        pos += kind
    ws = [refs[pos + i][...] for i in range(n_w)]
    pos += n_w
    emn = [refs[pos + i][...] for i in range(n_emn)]
    pos += n_emn
    en = [refs[pos + i][...] for i in range(n_en)]
    pos += n_en
    accs = [jnp.dot(x, w, preferred_element_type=_F32) for x, w in zip(xs, ws)]
    outs = epilogue(accs, emn, en)
    for o_ref, o in zip(refs[pos:], outs):
        o_ref[...] = o.astype(o_ref.dtype)


def _mm_tiles(m, n, x_row_bytes, k_total, n_emn, out_bytes):
    best = None
    for tm in (1024, 512, 256, 128, 64, 32, 16, 8):
        if tm > m or m % tm:
            continue
        for tn in (1024, 512, 256, 128):
            if tn > n or n % tn:
                continue
            need = 2 * (tm * x_row_bytes + k_total * tn * 2 + tm * tn * (4 * n_emn + out_bytes)) + tm * tn * 4
            if need > _VMEM_BUDGET:
                continue
            score = (tm * tn / (tm + tn), tm)
            if best is None or score > best[0]:
                best = (score, tm, tn)
    assert best is not None, (m, n)
    return best[1], best[2]


def _mm(xs, ws, *, epilogue, out_dtypes, extras_mn=(), extras_n=(), name="mm"):
    x_kinds = tuple(2 if isinstance(x, tuple) else 1 for x in xs)
    flat_x = [a for x in xs for a in (x if isinstance(x, tuple) else (x,))]
    m = flat_x[0].shape[0]
    n = ws[0].shape[1]
    x_row_bytes = sum(a.shape[1] * a.dtype.itemsize for a in flat_x)
    k_total = sum(w.shape[0] for w in ws)
    out_bytes = sum(jnp.dtype(dt).itemsize for dt in out_dtypes)
    tm, tn = _mm_tiles(m, n, x_row_bytes, k_total, len(extras_mn), out_bytes)
    in_specs = [pl.BlockSpec((tm, a.shape[1]), lambda i, j: (i, 0)) for a in flat_x]
    in_specs += [pl.BlockSpec((w.shape[0], tn), lambda i, j: (0, j)) for w in ws]
    in_specs += [pl.BlockSpec((tm, tn), lambda i, j: (i, j)) for _ in extras_mn]
    in_specs += [pl.BlockSpec((1, tn), lambda i, j: (0, j)) for _ in extras_n]
    kern = functools.partial(_mm_kernel, x_kinds=x_kinds, n_w=len(ws), n_emn=len(extras_mn),
                             n_en=len(extras_n), epilogue=epilogue)
    outs = pl.pallas_call(
        kern,
        grid=(m // tm, n // tn),
        in_specs=in_specs,
        out_specs=[pl.BlockSpec((tm, tn), lambda i, j: (i, j)) for _ in out_dtypes],
        out_shape=[jax.ShapeDtypeStruct((m, n), dt) for dt in out_dtypes],
        compiler_params=_params("arbitrary", "arbitrary"),
        name=name,
    )(*flat_x, *ws, *extras_mn, *[e.reshape(1, n) for e in extras_n])
    return outs


def _lora_kernel(x_ref, w1_ref, w2_ref, *refs, act, n_en, epilogue):
    h = jnp.dot(x_ref[...], w1_ref[...], preferred_element_type=_F32)
    acc = jnp.dot(act(h).astype(_BF16), w2_ref[...], preferred_element_type=_F32)
    en = [refs[i][...] for i in range(n_en)]
    refs[n_en][...] = epilogue(acc, en).astype(refs[n_en].dtype)


def _lora(x, w1, w2, *, act, epilogue, extras_n=(), name="lora"):
    m, k = x.shape
    r = w1.shape[1]
    n = w2.shape[1]
    rp = -(-r // _LANES) * _LANES
    w1 = jnp.pad(w1, ((0, 0), (0, rp - r))).astype(_BF16)
    w2 = jnp.pad(w2, ((0, rp - r), (0, 0))).astype(_BF16)
    tm = _divisor_tile(m, 512, 16)
    kern = functools.partial(_lora_kernel, act=act, n_en=len(extras_n), epilogue=epilogue)
    return pl.pallas_call(
        kern,
        grid=(m // tm,),
        in_specs=[pl.BlockSpec((tm, k), lambda i: (i, 0)),
                  pl.BlockSpec((k, rp), lambda i: (0, 0)),
                  pl.BlockSpec((rp, n), lambda i: (0, 0))]
                 + [pl.BlockSpec((1, n), lambda i: (0, 0)) for _ in extras_n],
        out_specs=pl.BlockSpec((tm, n), lambda i: (i, 0)),
        out_shape=jax.ShapeDtypeStruct((m, n), _F32),
        compiler_params=_params("arbitrary"),
        name=name,
    )(x, w1, w2, *[e.reshape(1, n) for e in extras_n])


def _scan_kernel(r_ref, w_ref, k_ref, v_ref, kk_ref, a_ref, rk_ref, gw_ref, gb_ref, s0_ref,
                 o_ref, s_ref, kk_s, kka_s, *, tb, n):
    @pl.when(pl.program_id(1) == 0)
    def _():
        s_ref[...] = s0_ref[...]

    rk = rk_ref[...]
    gw = gw_ref[...]
    gb = gb_ref[...]

    def step(t, carry):
        kk = kk_ref[t]
        kk = kk * lax.rsqrt(jnp.maximum(jnp.sum(kk * kk, axis=0, keepdims=True), _KK_EPS))
        kk_s[...] = kk
        kka_s[...] = kk * a_ref[t]
        v_t = v_ref[t]
        sk = [jnp.zeros_like(v_t), jnp.zeros_like(v_t)]
        for j in range(n):
            sk[j % 2] = sk[j % 2] + s_ref[j] * kk_s[j:j + 1, :]
        sk = sk[0] + sk[1]
        o = [jnp.zeros_like(v_t), jnp.zeros_like(v_t)]
        for j in range(n):
            s_new = (s_ref[j] * w_ref[t, j:j + 1, :]
                     + (v_t * k_ref[t, j:j + 1, :] - sk * kka_s[j:j + 1, :]))
            s_ref[j] = s_new
            o[j % 2] = o[j % 2] + s_new * r_ref[t, j:j + 1, :]
        o = o[0] + o[1]
        mean = jnp.mean(o, axis=0, keepdims=True)
        var = jnp.mean(jnp.square(o - mean), axis=0, keepdims=True)
        o = (o - mean) * lax.rsqrt(var + _GN_EPS) * gw + gb
        bonus = jnp.sum(r_ref[t] * k_ref[t] * rk, axis=0, keepdims=True) * v_t
        o_ref[t] = o + bonus
        return carry

    lax.fori_loop(0, tb, step, 0)


def _rwkv_scan(seqs, rk, gw, gb, s0):
    l, n, p = seqs[0].shape
    tb = _divisor_tile(l, 32, 1)
    seq_spec = pl.BlockSpec((tb, n, _LANES), lambda c, i: (i, 0, c))
    par_spec = pl.BlockSpec((n, _LANES), lambda c, i: (0, c))
    st_spec = pl.BlockSpec((n, n, _LANES), lambda c, i: (0, 0, c))
    return pl.pallas_call(
        functools.partial(_scan_kernel, tb=tb, n=n),
        grid=(p // _LANES, l // tb),
        in_specs=[seq_spec] * 6 + [par_spec] * 3 + [st_spec],
        out_specs=[seq_spec, st_spec],
        out_shape=[jax.ShapeDtypeStruct((l, n, p), _F32), jax.ShapeDtypeStruct((n, n, p), _F32)],
        scratch_shapes=[pltpu.VMEM((n, _LANES), _F32), pltpu.VMEM((n, _LANES), _F32)],
        compiler_params=_params("arbitrary", "arbitrary"),
        name="rwkv_scan",
    )(*seqs, rk, gw, gb, s0)


def _to_lanes(x, b, l, h, n, p):
    y = x.reshape(b, l, h, n).transpose(1, 3, 0, 2).reshape(l, n, b * h)
    return jnp.pad(y, ((0, 0), (0, 0), (0, p - b * h)))


def _from_lanes(y, b, l, h, n):
    return y[:, :, :b * h].reshape(l, n, b, h).transpose(2, 0, 3, 1).reshape(b * l, h * n)


def _param_lanes(w, b, h, n, p):
    y = jnp.broadcast_to(w.reshape(h, n).T[:, None, :], (n, b, h)).reshape(n, b * h)
    return jnp.pad(y, ((0, 0), (0, p - b * h)))


def _cumsum_kernel(x_ref, o_ref):
    x = x_ref[...]
    length = x.shape[1]
    col = lax.broadcasted_iota(jnp.int32, x.shape, 1)
    if length % _LANES == 0:
        s = 1
        while s < length:
            x = x + jnp.where(col >= s, pltpu.roll(x, s, axis=1), 0.0)
            s *= 2
        o_ref[...] = x
    else:
        acc = jnp.zeros_like(x)
        for s in range(length):
            acc = acc + jnp.where(col >= s, x[:, s:s + 1], 0.0)
        o_ref[...] = acc


def _cumsum_rows(x):
    return pl.pallas_call(
        _cumsum_kernel,
        out_shape=jax.ShapeDtypeStruct(x.shape, _F32),
        compiler_params=pltpu.CompilerParams(vmem_limit_bytes=_VMEM_LIMIT),
        name="logf_cumsum",
    )(x)


def _fox_kernel(q_ref, k_ref, v_ref, c_ref, o_ref, *, tq, scale):
    i = pl.program_id(2)
    q = q_ref[0]
    row = lax.broadcasted_iota(jnp.int32, (tq, tq), 0)
    col = lax.broadcasted_iota(jnp.int32, (tq, tq), 1)
    q0 = pl.multiple_of(i * tq, tq)
    cq_row = c_ref[0, :, pl.ds(q0, tq)]
    cq = jnp.sum(jnp.where(row == col, cq_row, 0.0), axis=1, keepdims=True)

    def block(j, carry, masked):
        m, l, acc = carry
        k0 = pl.multiple_of(j * tq, tq)
        s = lax.dot_general(q, k_ref[0, pl.ds(k0, tq), :], _NT, preferred_element_type=_F32) * scale
        s = s + (cq - c_ref[0, :, pl.ds(k0, tq)])
        if masked:
            s = jnp.where(col <= row, s, _NEG)
        m_new = jnp.maximum(m, jnp.max(s, axis=1, keepdims=True))
        alpha = jnp.exp(m - m_new)
        p = jnp.exp(s - m_new)
        l = alpha * l + jnp.sum(p, axis=1, keepdims=True)
        acc = alpha * acc + jnp.dot(p.astype(_BF16), v_ref[0, pl.ds(k0, tq), :], preferred_element_type=_F32)
        return m_new, l, acc

    init = (jnp.full((tq, 1), _NEG, _F32), jnp.zeros((tq, 1), _F32), jnp.zeros((tq, q.shape[1]), _F32))
    carry = lax.fori_loop(0, i, lambda j, c: block(j, c, False), init)
    _, l, acc = block(i, carry, True)
    o_ref[0] = (acc / l).astype(o_ref.dtype)


def _fox_attention(q, k, v, c_rows, b, l, h, dh):
    tq = _divisor_tile(l, 256, _LANES)
    qo_spec = pl.BlockSpec((1, tq, dh), lambda bi, hi, i: (bi, i, hi))
    kv_spec = pl.BlockSpec((1, l, dh), lambda bi, hi, i: (bi, 0, hi))
    return pl.pallas_call(
        functools.partial(_fox_kernel, tq=tq, scale=dh ** -0.5),
        grid=(b, h, l // tq),
        in_specs=[qo_spec, kv_spec, kv_spec, pl.BlockSpec((1, 1, l), lambda bi, hi, i: (bi * h + hi, 0, 0))],
        out_specs=qo_spec,
        out_shape=jax.ShapeDtypeStruct((b, l, h * dh), _BF16),
        compiler_params=_params("arbitrary", "arbitrary", "arbitrary"),
        name="fox_attention",
    )(q, k, v, c_rows)


def _decode_kernel(pt_ref, qbd_ref, cq_ref, ckn_ref, kn_ref, vn_ref, kp_ref, vp_ref, lf_ref, et_ref,
                   o_ref, m_s, l_s, acc_s, carry_s, *, n_heads, n_q, dh, scale, n_pages):
    del pt_ref
    p = pl.program_id(1)
    qbd = qbd_ref[0]

    def attend(kb, vb, bias, mask):
        s = lax.dot_general(qbd, kb, _NT, preferred_element_type=_F32) * scale + bias
        if mask is not None:
            s = jnp.where(mask, s, _NEG)
        m_old = m_s[...]
        m_new = jnp.maximum(m_old, jnp.max(s, axis=1, keepdims=True))
        alpha = jnp.exp(m_old - m_new)
        pr = jnp.exp(s - m_new)
        l_s[...] = alpha * l_s[...] + jnp.sum(pr, axis=1, keepdims=True)
        m_s[...] = m_new
        acc_s[...] = alpha * acc_s[...] + jnp.dot(pr.astype(_BF16), vb, preferred_element_type=_F32)

    @pl.when(p == 0)
    def _():
        m_s[...] = jnp.full(m_s.shape, _NEG, _F32)
        l_s[...] = jnp.zeros(l_s.shape, _F32)
        acc_s[...] = jnp.zeros(acc_s.shape, _F32)
        carry_s[...] = jnp.zeros(carry_s.shape, _F32)
        shape = ckn_ref.shape[1:]
        row = lax.broadcasted_iota(jnp.int32, shape, 0)
        col = lax.broadcasted_iota(jnp.int32, shape, 1)
        attend(kn_ref[0], vn_ref[0], cq_ref[0] - ckn_ref[0], col <= lax.rem(row, n_q))

    @pl.when(p > 0)
    def _():
        lfx = lax.dot_general(et_ref[...], lf_ref[...], _NT, precision=_HI, preferred_element_type=_F32)
        ps = lfx.shape[1]
        jj = lax.broadcasted_iota(jnp.int32, (ps, ps), 0)
        ss = lax.broadcasted_iota(jnp.int32, (ps, ps), 1)
        later = jnp.where(jj > ss, 1.0, 0.0).astype(_F32)
        suffix = jnp.dot(lfx, later, precision=_HI, preferred_element_type=_F32)
        bias = cq_ref[0] + suffix + carry_s[...]
        carry_s[...] = carry_s[...] + jnp.sum(lfx, axis=1, keepdims=True)
        attend(kp_ref[...].astype(_BF16), vp_ref[...].astype(_BF16), bias, None)

    @pl.when(p == n_pages)
    def _():
        l = l_s[...]
        for hh in range(n_heads):
            rows = slice(hh * n_q, (hh + 1) * n_q)
            cols = slice(hh * dh, (hh + 1) * dh)
            o_ref[0, :, cols] = (acc_s[rows, cols] / l[rows]).astype(o_ref.dtype)


def _fox_decode(q, k_new, v_new, c_new, cache_k, cache_v, cache_logf, page_table, layer):
    b, nq, h, dh = q.shape
    n_phys, n_fox, ps = cache_k.shape[:3]
    n_pages = page_table.shape[1]
    hd = h * dh
    r = h * nq
    eye = jnp.eye(h, dtype=q.dtype)
    qbd = (q.transpose(0, 2, 1, 3)[:, :, :, None, :] * eye[None, :, None, :, None]).reshape(b, r, hd).astype(_BF16)
    pad_rows = lambda x: jnp.pad(x.reshape(b, nq, hd), ((0, 0), (0, ps - nq), (0, 0))).astype(_BF16)
    c_hq = c_new.transpose(0, 2, 1)
    cq = c_hq.reshape(b, r, 1)
    ckn = jnp.pad(jnp.broadcast_to(c_hq[:, :, None, :], (b, h, nq, nq)).reshape(b, r, nq),
                  ((0, 0), (0, 0), (0, ps - nq)))
    expand = jnp.repeat(jnp.eye(h, dtype=_F32), nq, axis=0)
    page_of = lambda bi, p, pt: pt[bi, n_pages - jnp.maximum(p, 1)]
    per_seq = lambda shape: pl.BlockSpec((1,) + shape, lambda bi, p, pt: (bi, 0, 0))
    page_spec = lambda last: pl.BlockSpec((None, None, ps, last), lambda bi, p, pt: (page_of(bi, p, pt), layer, 0, 0))
    grid_spec = pltpu.PrefetchScalarGridSpec(
        num_scalar_prefetch=1,
        grid=(b, n_pages + 1),
        in_specs=[per_seq((r, hd)), per_seq((r, 1)), per_seq((r, ps)), per_seq((ps, hd)), per_seq((ps, hd)),
                  page_spec(hd), page_spec(hd), page_spec(h),
                  pl.BlockSpec((r, h), lambda bi, p, pt: (0, 0))],
        out_specs=per_seq((nq, hd)),
        scratch_shapes=[pltpu.VMEM((r, 1), _F32), pltpu.VMEM((r, 1), _F32), pltpu.VMEM((r, hd), _F32),
                        pltpu.VMEM((r, 1), _F32)],
    )
    out = pl.pallas_call(
        functools.partial(_decode_kernel, n_heads=h, n_q=nq, dh=dh, scale=dh ** -0.5, n_pages=n_pages),
        grid_spec=grid_spec,
        out_shape=jax.ShapeDtypeStruct((b, nq, hd), _F32),
        compiler_params=_params("arbitrary", "arbitrary"),
        name="fox_decode",
    )(page_table, qbd, cq, ckn, pad_rows(k_new), pad_rows(v_new),
      cache_k.reshape(n_phys, n_fox, ps, hd), cache_v.reshape(n_phys, n_fox, ps, hd), cache_logf, expand)
    return out.reshape(b * nq, hd).astype(_BF16)


def _first(accs, emn, en):
    return [accs[0]]


def _rwkv_layer(x, shift_prev, wkv_prev, v_first, j, i, W, Wb):
    b, l, d = x.shape
    h, n = wkv_prev.shape[1], wkv_prev.shape[2]
    xr, xw, xk, xv, xa, xg = _rwkv_mix(x, shift_prev, W['norm_mix'][i], W['rwkv_mu'][j])
    (r,) = _mm([xr], [Wb['rwkv_w_r'][j]], epilogue=_first, out_dtypes=[_F32], name="rwkv_r")
    decay = _lora(xw, W['rwkv_w1'][j], W['rwkv_w2'][j], act=jnp.tanh, extras_n=[W['rwkv_w0'][j]],
                  epilogue=lambda acc, en: jnp.exp(-jnp.exp(-_softplus(-(en[0] + acc)) - 0.5)), name="rwkv_decay")
    a = _lora(xa, W['rwkv_a1'][j], W['rwkv_a2'][j], act=lambda t: t, extras_n=[W['rwkv_a0'][j]],
              epilogue=lambda acc, en: _sigmoid(en[0] + acc), name="rwkv_a")
    g = _lora(xg, W['rwkv_g1'][j], W['rwkv_g2'][j], act=_sigmoid, epilogue=lambda acc, en: acc, name="rwkv_g")
    if j == 0:
        (v,) = _mm([xv], [Wb['rwkv_w_v'][j]], epilogue=_first, out_dtypes=[_F32], name="rwkv_v")
        v_first = v
    else:
        gate = _lora(xv, W['rwkv_v1'][j - 1], W['rwkv_v2'][j - 1], act=lambda t: t, extras_n=[W['rwkv_v0'][j - 1]],
                     epilogue=lambda acc, en: _sigmoid(en[0] + acc), name="rwkv_vgate")
        (v,) = _mm([xv], [Wb['rwkv_w_v'][j]], extras_mn=[v_first, gate], out_dtypes=[_F32], name="rwkv_v",
                   epilogue=lambda accs, emn, en: [accs[0] + (emn[0] - accs[0]) * emn[1]])
    kk_raw, k_mod = _mm([xk], [Wb['rwkv_w_k'][j]], extras_mn=[a], extras_n=[W['rwkv_k_k'][j], W['rwkv_k_a'][j]],
                        out_dtypes=[_F32, _F32], name="rwkv_k",
                        epilogue=lambda accs, emn, en: [accs[0] * en[0], accs[0] * (1.0 + (emn[0] - 1.0) * en[1])])
    p = -(-(b * h) // _LANES) * _LANES
    seqs = [_to_lanes(t, b, l, h, n, p) for t in (r, decay, k_mod, v, kk_raw, a)]
    s0 = jnp.pad(wkv_prev.astype(_F32).transpose(3, 2, 0, 1).reshape(n, n, b * h), ((0, 0), (0, 0), (0, p - b * h)))
    o, s_last = _rwkv_scan(seqs, _param_lanes(W['rwkv_r_k'][j].reshape(-1), b, h, n, p),
                           _param_lanes(W['rwkv_gn_w'][j], b, h, n, p), _param_lanes(W['rwkv_gn_b'][j], b, h, n, p), s0)
    o = _from_lanes(o, b, l, h, n)
    s_last = s_last[:, :, :b * h].reshape(n, n, b, h).transpose(2, 3, 1, 0)
    x2 = x.reshape(b * l, d)
    (x2,) = _mm([(o, g)], [Wb['rwkv_w_o'][j]], extras_mn=[x2], out_dtypes=[_F32], name="rwkv_out",
                epilogue=lambda accs, emn, en: [emn[0] + accs[0]])
    shift_out = _rmsnorm(x[:, -1, :], W['norm_mix'][i], _F32)
    return x2.reshape(b, l, d), shift_out, s_last, v_first


def _fox_layer(x, j, i, W, Wb, paged):
    b, l, d = x.shape
    h = W['fox_b_f'].shape[1]
    dh = d // h
    x2 = x.reshape(b * l, d)
    hn = _rmsnorm(x2, W['norm_mix'][i], _BF16)
    w_q, w_k, w_v, w_f = Wb['fox_in'][j]
    (q,) = _mm([hn], [w_q], epilogue=_first, out_dtypes=[_BF16 if paged is None else _F32], name="fox_q")
    k, kb = _mm([hn], [w_k], epilogue=lambda accs, emn, en: [accs[0], accs[0]], out_dtypes=[_F32, _BF16], name="fox_k")
    v, vb = _mm([hn], [w_v], epilogue=lambda accs, emn, en: [accs[0], accs[0]], out_dtypes=[_F32, _BF16], name="fox_v")
    bias_f = jnp.pad(W['fox_b_f'][j], (0, w_f.shape[1] - h))
    (logf_pad,) = _mm([hn], [w_f], extras_n=[bias_f], out_dtypes=[_F32], name="fox_logf",
                      epilogue=lambda accs, emn, en: [-_softplus(-(accs[0] + en[0]))])
    logf = logf_pad[:, :h].reshape(b, l, h)
    c_rows = _cumsum_rows(logf.transpose(0, 2, 1).reshape(b * h, l))
    if paged is None:
        o = _fox_attention(q.reshape(b, l, d), kb.reshape(b, l, d), vb.reshape(b, l, d),
                           c_rows.reshape(b * h, 1, l), b, l, h, dh)
        o = o.reshape(b * l, d)
    else:
        cache_k, cache_v, cache_logf, page_table = paged
        c_new = c_rows.reshape(b, h, l).transpose(0, 2, 1)
        o = _fox_decode(q.reshape(b, l, h, dh), k.reshape(b, l, h, dh), v.reshape(b, l, h, dh), c_new,
                        cache_k, cache_v, cache_logf, page_table, j)
    (x2,) = _mm([o], [Wb['fox_w_o'][j]], extras_mn=[x2], out_dtypes=[_F32], name="fox_out",
                epilogue=lambda accs, emn, en: [emn[0] + accs[0]])
    return x2.reshape(b, l, d), k.reshape(b, l, h, dh), v.reshape(b, l, h, dh), logf


def _ffn_ple(x, p_i, i, W, Wb):
    b, l, d = x.shape
    x2 = x.reshape(b * l, d)
    hn = _rmsnorm(x2, W['norm_ffn'][i], _BF16)
    (up,) = _mm([hn], [Wb['w_up'][i]], out_dtypes=[_BF16], name="ffn_up",
                epilogue=lambda accs, emn, en: [jnp.square(jnp.maximum(accs[0], 0.0))])
    (x2,) = _mm([up], [Wb['w_down'][i]], extras_mn=[x2], out_dtypes=[_F32], name="ffn_down",
                epilogue=lambda accs, emn, en: [emn[0] + accs[0]])
    hp = _rmsnorm(x2, W['norm_ple'][i], _BF16)
    pb = p_i.reshape(b * l, -1).astype(_BF16)
    (x2,) = _mm([hp, pb], [Wb['w_ple_gate'][i], Wb['w_ple'][i]], extras_mn=[x2], out_dtypes=[_F32], name="ple",
                epilogue=lambda accs, emn, en: [emn[0] + _sigmoid(accs[0]) * accs[1]])
    return x2.reshape(b, l, d)


def _trunk(x, p, shift_in, wkv_in, paged, W, Wb):
    depth = W['norm_mix'].shape[0]
    v_first = None
    shifts, wkvs, ks, vs, lfs = [], [], [], [], []
    for i in range(depth):
        j = i // 2
        if i % 2 == 0:
            x, sh, s_last, v_first = _rwkv_layer(x, shift_in[j], wkv_in[j], v_first, j, i, W, Wb)
            shifts.append(sh)
            wkvs.append(s_last)
        else:
            x, k, v, lf = _fox_layer(x, j, i, W, Wb, paged)
            ks.append(k)
            vs.append(v)
            lfs.append(lf)
        x = _ffn_ple(x, p[i], i, W, Wb)
    b, l, d = x.shape
    y = _rmsnorm(x.reshape(b * l, d), W['norm_out'], _F32).reshape(b, l, d)
    return (y, jnp.stack(shifts, 0), jnp.stack(wkvs, 0),
            jnp.stack(ks, 1), jnp.stack(vs, 1), jnp.stack(lfs, 1))


def kernel(x_prompt, x_sample, cache_k, cache_v, cache_logf, state_wkv, state_shift, page_table, p_prompt, p_sample, norm_mix, norm_ffn, norm_ple, norm_out, rwkv_mu, rwkv_w_r, rwkv_w_k, rwkv_w_v, rwkv_w_o, rwkv_w0, rwkv_w1, rwkv_w2, rwkv_a0, rwkv_a1, rwkv_a2, rwkv_v0, rwkv_v1, rwkv_v2, rwkv_g1, rwkv_g2, rwkv_k_k, rwkv_k_a, rwkv_r_k, rwkv_gn_w, rwkv_gn_b, fox_w_in, fox_b_f, fox_w_o, w_up, w_down, w_ple, w_ple_gate):
    W = {
        'norm_mix': norm_mix, 'norm_ffn': norm_ffn, 'norm_ple': norm_ple, 'norm_out': norm_out,
        'rwkv_mu': rwkv_mu, 'rwkv_w0': rwkv_w0, 'rwkv_w1': rwkv_w1, 'rwkv_w2': rwkv_w2,
        'rwkv_a0': rwkv_a0, 'rwkv_a1': rwkv_a1, 'rwkv_a2': rwkv_a2,
        'rwkv_v0': rwkv_v0, 'rwkv_v1': rwkv_v1, 'rwkv_v2': rwkv_v2,
        'rwkv_g1': rwkv_g1, 'rwkv_g2': rwkv_g2, 'rwkv_k_k': rwkv_k_k, 'rwkv_k_a': rwkv_k_a,
        'rwkv_r_k': rwkv_r_k, 'rwkv_gn_w': rwkv_gn_w, 'rwkv_gn_b': rwkv_gn_b, 'fox_b_f': fox_b_f,
    }
    d = x_prompt.shape[-1]
    n_fox_heads = fox_b_f.shape[1]
    f_pad = -(-n_fox_heads // _LANES) * _LANES - n_fox_heads
    bf = lambda w: w.astype(_BF16)
    Wb = {
        'rwkv_w_r': bf(rwkv_w_r), 'rwkv_w_k': bf(rwkv_w_k), 'rwkv_w_v': bf(rwkv_w_v), 'rwkv_w_o': bf(rwkv_w_o),
        'fox_in': [(bf(w[:, :d]), bf(w[:, d:2 * d]), bf(w[:, 2 * d:3 * d]),
                    bf(jnp.pad(w[:, 3 * d:], ((0, 0), (0, f_pad))))) for w in fox_w_in],
        'fox_w_o': bf(fox_w_o), 'w_up': bf(w_up), 'w_down': bf(w_down),
        'w_ple': bf(w_ple), 'w_ple_gate': bf(w_ple_gate),
    }
    n_rwkv, _, n_heads, n_dim, _ = state_wkv.shape
    b = x_prompt.shape[0]
    shift0 = jnp.zeros((n_rwkv, b, d), x_prompt.dtype)
    wkv0 = jnp.zeros((n_rwkv, b, n_heads, n_dim, n_dim), _F32)
    y_p, shift_p, wkv_p, k_p, v_p, lf_p = _trunk(x_prompt, p_prompt, shift0, wkv0, None, W, Wb)
    y_s, shift_s, wkv_s, k_s, v_s, lf_s = _trunk(
        x_sample, p_sample, state_shift, state_wkv, (cache_k, cache_v, cache_logf, page_table), W, Wb)
    return (y_p, y_s, k_p, v_p, lf_p, k_s, v_s, lf_s, wkv_p, shift_p, wkv_s, shift_s)
```

```python
import functools

import jax
import jax.numpy as jnp
from jax import lax
from jax.experimental import pallas as pl
from jax.experimental.pallas import tpu as pltpu

_RMS_EPS = 1e-6
_GN_EPS = 64e-5
_KK_EPS = 1e-24
_NEG = -1e30
_LANES = 128
_VMEM_LIMIT = 52 * 1024 * 1024
_VMEM_BUDGET = 36 * 1024 * 1024
_F32 = jnp.float32
_BF16 = jnp.bfloat16
_HI = lax.Precision.HIGHEST
_NT = (((1,), (1,)), ((), ()))


def _params(*sem):
    return pltpu.CompilerParams(dimension_semantics=sem, vmem_limit_bytes=_VMEM_LIMIT)


def _divisor_tile(n, target, align):
    t = min(n, target)
    t -= t % align
    while t >= align:
        if n % t == 0:
            return t
        t -= align
    return n


def _sigmoid(x):
    return 1.0 / (1.0 + jnp.exp(-x))


def _softplus(x):
    return jnp.maximum(x, 0.0) + jnp.log(1.0 + jnp.exp(-jnp.abs(x)))


def _rms(x, g):
    return x * lax.rsqrt(jnp.mean(x * x, axis=-1, keepdims=True) + _RMS_EPS) * g


def _rmsnorm_kernel(x_ref, g_ref, o_ref):
    o_ref[...] = _rms(x_ref[...], g_ref[...]).astype(o_ref.dtype)


def _rmsnorm(x, g, out_dtype):
    t, d = x.shape
    tr = _divisor_tile(t, 512, 16)
    return pl.pallas_call(
        _rmsnorm_kernel,
        grid=(t // tr,),
        in_specs=[pl.BlockSpec((tr, d), lambda i: (i, 0)), pl.BlockSpec((1, d), lambda i: (0, 0))],
        out_specs=pl.BlockSpec((tr, d), lambda i: (i, 0)),
        out_shape=jax.ShapeDtypeStruct((t, d), out_dtype),
        compiler_params=_params("arbitrary"),
        name="rmsnorm",
    )(x, g.reshape(1, d))


def _mix_kernel(x_ref, xp_ref, sh_ref, g_ref, mu_ref, *o_refs):
    g = g_ref[...]
    xn = _rms(x_ref[0], g)
    prev = _rms(xp_ref[0, 7:8, :], g)
    prev = jnp.where(pl.program_id(1) == 0, sh_ref[0], prev)
    row = lax.broadcasted_iota(jnp.int32, xn.shape, 0)
    xprev = jnp.where(row == 0, prev, pltpu.roll(xn, 1, axis=0))
    xx = xprev - xn
    for c, o_ref in enumerate(o_refs):
        o_ref[0] = (xn + xx * mu_ref[c:c + 1, :]).astype(o_ref.dtype)


def _rwkv_mix(x, shift_prev, g, mu):
    b, l, d = x.shape
    tr = _divisor_tile(l, 256, 8)
    nb8 = tr // 8
    outs = pl.pallas_call(
        _mix_kernel,
        grid=(b, l // tr),
        in_specs=[
            pl.BlockSpec((1, tr, d), lambda bi, i: (bi, i, 0)),
            pl.BlockSpec((1, 8, d), lambda bi, i: (bi, jnp.maximum(i * nb8 - 1, 0), 0)),
            pl.BlockSpec((1, 1, d), lambda bi, i: (bi, 0, 0)),
            pl.BlockSpec((1, d), lambda bi, i: (0, 0)),
            pl.BlockSpec((6, d), lambda bi, i: (0, 0)),
        ],
        out_specs=[pl.BlockSpec((1, tr, d), lambda bi, i: (bi, i, 0))] * 6,
        out_shape=[jax.ShapeDtypeStruct((b, l, d), _BF16)] * 6,
        compiler_params=_params("arbitrary", "arbitrary"),
        name="rwkv_mix",
    )(x, x, shift_prev.reshape(b, 1, d), g.reshape(1, d), mu)
    return [o.reshape(b * l, d) for o in outs]


def _mm_kernel(*refs, x_kinds, n_w, n_emn, n_en, epilogue):
    pos = 0
    xs = []
    for kind in x_kinds:
        if kind == 1:
            xs.append(refs[pos][...])
        else:
            xs.append((refs[pos][...] * refs[pos + 1][...]).astype(_BF16))
        pos += kind
    ws = [refs[pos + i][...] for i in range(n_w)]
    pos += n_w
    emn = [refs[pos + i][...] for i in range(n_emn)]
    pos += n_emn
    en = [refs[pos + i][...] for i in range(n_en)]
    pos += n_en
    accs = [jnp.dot(x, w, preferred_element_type=_F32) for x, w in zip(xs, ws)]
    outs = epilogue(accs, emn, en)
    for o_ref, o in zip(refs[pos:], outs):
        o_ref[...] = o.astype(o_ref.dtype)


def _mm_tiles(m, n, x_row_bytes, k_total, n_emn, out_bytes):
    best = None
    for tm in (1024, 512, 256, 128, 64, 32, 16, 8):
        if tm > m or m % tm:
            continue
        for tn in (1024, 512, 256, 128):
            if tn > n or n % tn:
                continue
            need = 2 * (tm * x_row_bytes + k_total * tn * 2 + tm * tn * (4 * n_emn + out_bytes)) + tm * tn * 4
            if need > _VMEM_BUDGET:
                continue
            score = (tm * tn / (tm + tn), tm)
            if best is None or score > best[0]:
                best = (score, tm, tn)
    assert best is not None, (m, n)
    return best[1], best[2]


def _mm(xs, ws, *, epilogue, out_dtypes, extras_mn=(), extras_n=(), name="mm"):
    x_kinds = tuple(2 if isinstance(x, tuple) else 1 for x in xs)
    flat_x = [a for x in xs for a in (x if isinstance(x, tuple) else (x,))]
    m = flat_x[0].shape[0]
    n = ws[0].shape[1]
    x_row_bytes = sum(a.shape[1] * a.dtype.itemsize for a in flat_x)
    k_total = sum(w.shape[0] for w in ws)
    out_bytes = sum(jnp.dtype(dt).itemsize for dt in out_dtypes)
    tm, tn = _mm_tiles(m, n, x_row_bytes, k_total, len(extras_mn), out_bytes)
    in_specs = [pl.BlockSpec((tm, a.shape[1]), lambda i, j: (i, 0)) for a in flat_x]
    in_specs += [pl.BlockSpec((w.shape[0], tn), lambda i, j: (0, j)) for w in ws]
    in_specs += [pl.BlockSpec((tm, tn), lambda i, j: (i, j)) for _ in extras_mn]
    in_specs += [pl.BlockSpec((1, tn), lambda i, j: (0, j)) for _ in extras_n]
    kern = functools.partial(_mm_kernel, x_kinds=x_kinds, n_w=len(ws), n_emn=len(extras_mn),
                             n_en=len(extras_n), epilogue=epilogue)
    outs = pl.pallas_call(
        kern,
        grid=(m // tm, n // tn),
        in_specs=in_specs,
        out_specs=[pl.BlockSpec((tm, tn), lambda i, j: (i, j)) for _ in out_dtypes],
        out_shape=[jax.ShapeDtypeStruct((m, n), dt) for dt in out_dtypes],
        compiler_params=_params("arbitrary", "arbitrary"),
        name=name,
    )(*flat_x, *ws, *extras_mn, *[e.reshape(1, n) for e in extras_n])
    return outs


def _lora_kernel(x_ref, w1_ref, w2_ref, *refs, act, n_en, epilogue):
    h = jnp.dot(x_ref[...], w1_ref[...], preferred_element_type=_F32)
    acc = jnp.dot(act(h).astype(_BF16), w2_ref[...], preferred_element_type=_F32)
    en = [refs[i][...] for i in range(n_en)]
    refs[n_en][...] = epilogue(acc, en).astype(refs[n_en].dtype)


def _lora(x, w1, w2, *, act, epilogue, extras_n=(), name="lora"):
    m, k = x.shape
    r = w1.shape[1]
    n = w2.shape[1]
    rp = -(-r // _LANES) * _LANES
    w1 = jnp.pad(w1, ((0, 0), (0, rp - r))).astype(_BF16)
    w2 = jnp.pad(w2, ((0, rp - r), (0, 0))).astype(_BF16)
    tm = _divisor_tile(m, 512, 16)
    kern = functools.partial(_lora_kernel, act=act, n_en=len(extras_n), epilogue=epilogue)
    return pl.pallas_call(
        kern,
        grid=(m // tm,),
        in_specs=[pl.BlockSpec((tm, k), lambda i: (i, 0)),
                  pl.BlockSpec((k, rp), lambda i: (0, 0)),
                  pl.BlockSpec((rp, n), lambda i: (0, 0))]
                 + [pl.BlockSpec((1, n), lambda i: (0, 0)) for _ in extras_n],
        out_specs=pl.BlockSpec((tm, n), lambda i: (i, 0)),
        out_shape=jax.ShapeDtypeStruct((m, n), _F32),
        compiler_params=_params("arbitrary"),
        name=name,
    )(x, w1, w2, *[e.reshape(1, n) for e in extras_n])


def _scan_kernel(r_ref, w_ref, k_ref, v_ref, kk_ref, a_ref, rk_ref, gw_ref, gb_ref, s0_ref,
                 o_ref, s_ref, kk_s, kka_s, *, tb, n):
    @pl.when(pl.program_id(1) == 0)
    def _():
        s_ref[...] = s0_ref[...]

    rk = rk_ref[...]
    gw = gw_ref[...]
    gb = gb_ref[...]

    def step(t, carry):
        kk = kk_ref[t]
        kk = kk * lax.rsqrt(jnp.maximum(jnp.sum(kk * kk, axis=0, keepdims=True), _KK_EPS))
        kk_s[...] = kk
        kka_s[...] = kk * a_ref[t]
        v_t = v_ref[t]
        sk = [jnp.zeros_like(v_t), jnp.zeros_like(v_t)]
        for j in range(n):
            sk[j % 2] = sk[j % 2] + s_ref[j] * kk_s[j:j + 1, :]
        sk = sk[0] + sk[1]
        o = [jnp.zeros_like(v_t), jnp.zeros_like(v_t)]
        for j in range(n):
            s_new = (s_ref[j] * w_ref[t, j:j + 1, :]
                     + (v_t * k_ref[t, j:j + 1, :] - sk * kka_s[j:j + 1, :]))
            s_ref[j] = s_new
            o[j % 2] = o[j % 2] + s_new * r_ref[t, j:j + 1, :]
        o = o[0] + o[1]
        mean = jnp.mean(o, axis=0, keepdims=True)
        var = jnp.mean(jnp.square(o - mean), axis=0, keepdims=True)
        o = (o - mean) * lax.rsqrt(var + _GN_EPS) * gw + gb
        bonus = jnp.sum(r_ref[t] * k_ref[t] * rk, axis=0, keepdims=True) * v_t
        o_ref[t] = o + bonus
        return carry

    lax.fori_loop(0, tb, step, 0)


def _rwkv_scan(seqs, rk, gw, gb, s0):
    l, n, p = seqs[0].shape
    tb = _divisor_tile(l, 32, 1)
    seq_spec = pl.BlockSpec((tb, n, _LANES), lambda c, i: (i, 0, c))
    par_spec = pl.BlockSpec((n, _LANES), lambda c, i: (0, c))
    st_spec = pl.BlockSpec((n, n, _LANES), lambda c, i: (0, 0, c))
    return pl.pallas_call(
        functools.partial(_scan_kernel, tb=tb, n=n),
        grid=(p // _LANES, l // tb),
        in_specs=[seq_spec] * 6 + [par_spec] * 3 + [st_spec],
        out_specs=[seq_spec, st_spec],
        out_shape=[jax.ShapeDtypeStruct((l, n, p), _F32), jax.ShapeDtypeStruct((n, n, p), _F32)],
        scratch_shapes=[pltpu.VMEM((n, _LANES), _F32), pltpu.VMEM((n, _LANES), _F32)],
        compiler_params=_params("arbitrary", "arbitrary"),
        name="rwkv_scan",
    )(*seqs, rk, gw, gb, s0)


def _to_lanes(x, b, l, h, n, p):
    y = x.reshape(b, l, h, n).transpose(1, 3, 0, 2).reshape(l, n, b * h)
    return jnp.pad(y, ((0, 0), (0, 0), (0, p - b * h)))


def _from_lanes(y, b, l, h, n):
    return y[:, :, :b * h].reshape(l, n, b, h).transpose(2, 0, 3, 1).reshape(b * l, h * n)


def _param_lanes(w, b, h, n, p):
    y = jnp.broadcast_to(w.reshape(h, n).T[:, None, :], (n, b, h)).reshape(n, b * h)
    return jnp.pad(y, ((0, 0), (0, p - b * h)))


def _cumsum_kernel(x_ref, o_ref):
    x = x_ref[...]
    length = x.shape[1]
    col = lax.broadcasted_iota(jnp.int32, x.shape, 1)
    if length % _LANES == 0:
        s = 1
        while s < length:
            x = x + jnp.where(col >= s, pltpu.roll(x, s, axis=1), 0.0)
            s *= 2
        o_ref[...] = x
    else:
        acc = jnp.zeros_like(x)
        for s in range(length):
            acc = acc + jnp.where(col >= s, x[:, s:s + 1], 0.0)
        o_ref[...] = acc


def _cumsum_rows(x):
    return pl.pallas_call(
        _cumsum_kernel,
        out_shape=jax.ShapeDtypeStruct(x.shape, _F32),
        compiler_params=pltpu.CompilerParams(vmem_limit_bytes=_VMEM_LIMIT),
        name="logf_cumsum",
    )(x)


def _fox_kernel(q_ref, k_ref, v_ref, c_ref, o_ref, *, tq, dh, n_grp, scale):
    i = pl.program_id(2)
    row = lax.broadcasted_iota(jnp.int32, (tq, tq), 0)
    col = lax.broadcasted_iota(jnp.int32, (tq, tq), 1)
    q0 = pl.multiple_of(i * tq, tq)
    qs = [q_ref[0, :, g * dh:(g + 1) * dh] for g in range(n_grp)]
    cqs = [jnp.sum(jnp.where(row == col, c_ref[g, :, pl.ds(q0, tq)], 0.0), axis=1, keepdims=True)
           for g in range(n_grp)]

    def block(j, carry, masked):
        k0 = pl.multiple_of(j * tq, tq)
        out = []
        for g in range(n_grp):
            m, l, acc = carry[g]
            lanes = slice(g * dh, (g + 1) * dh)
            s = lax.dot_general(qs[g], k_ref[0, pl.ds(k0, tq), lanes], _NT, preferred_element_type=_F32)
            u = s * scale - c_ref[g, :, pl.ds(k0, tq)]
            if masked:
                u = jnp.where(col <= row, u, _NEG)
            m_new = jnp.maximum(m, jnp.max(u, axis=1, keepdims=True) + cqs[g])
            alpha = jnp.exp(m - m_new)
            p = jnp.exp(u - (m_new - cqs[g]))
            l = alpha * l + jnp.sum(p, axis=1, keepdims=True)
            acc = alpha * acc + jnp.dot(p.astype(_BF16), v_ref[0, pl.ds(k0, tq), lanes],
                                        preferred_element_type=_F32)
            out.append((m_new, l, acc))
        return tuple(out)

    init = tuple((jnp.full((tq, 1), _NEG, _F32), jnp.zeros((tq, 1), _F32), jnp.zeros((tq, dh), _F32))
                 for _ in range(n_grp))
    carry = lax.fori_loop(0, i, lambda j, c: block(j, c, False), init)
    carry = block(i, carry, True)
    for g in range(n_grp):
        _, l, acc = carry[g]
        o_ref[0, :, g * dh:(g + 1) * dh] = (acc / l).astype(o_ref.dtype)


def _fox_attention(q, k, v, c_rows, b, l, h, dh):
    tq = _divisor_tile(l, 512, _LANES)
    n_grp = 2 if h % 2 == 0 else 1
    qo_spec = pl.BlockSpec((1, tq, n_grp * dh), lambda bi, hi, i: (bi, i, hi))
    kv_spec = pl.BlockSpec((1, l, n_grp * dh), lambda bi, hi, i: (bi, 0, hi))
    c_spec = pl.BlockSpec((n_grp, 1, l), lambda bi, hi, i: (bi * (h // n_grp) + hi, 0, 0))
    return pl.pallas_call(
        functools.partial(_fox_kernel, tq=tq, dh=dh, n_grp=n_grp, scale=dh ** -0.5),
        grid=(b, h // n_grp, l // tq),
        in_specs=[qo_spec, kv_spec, kv_spec, c_spec],
        out_specs=qo_spec,
        out_shape=jax.ShapeDtypeStruct((b, l, h * dh), _BF16),
        compiler_params=_params("arbitrary", "arbitrary", "arbitrary"),
        name="fox_attention",
    )(q, k, v, c_rows)


def _suffix_kernel(x_ref, sfx_ref, tot_ref, *, stride):
    x = x_ref[...]
    n = x.shape[1]
    col = lax.broadcasted_iota(jnp.int32, x.shape, 1)
    inc = x
    tot = x
    s = stride
    while s < n:
        inc = inc + jnp.where(col + s < n, pltpu.roll(inc, n - s, axis=1), 0.0)
        tot = tot + pltpu.roll(tot, n - s, axis=1)
        s *= 2
    sfx_ref[...] = inc - x
    tot_ref[...] = tot


def _page_suffix(cache_logf):
    n_phys, n_fox, ps, h = cache_logf.shape
    assert ps & (ps - 1) == 0 and (ps * h) % _LANES == 0, (ps, h)
    rows = n_phys * n_fox
    tr = _divisor_tile(rows, 256, 8)
    spec = pl.BlockSpec((tr, ps * h), lambda i: (i, 0))
    sfx, tot = pl.pallas_call(
        functools.partial(_suffix_kernel, stride=h),
        grid=(rows // tr,),
        in_specs=[spec],
        out_specs=[spec, spec],
        out_shape=[jax.ShapeDtypeStruct((rows, ps * h), _F32)] * 2,
        compiler_params=_params("arbitrary"),
        name="page_suffix",
    )(cache_logf.reshape(rows, ps * h))
    return sfx.reshape(n_phys, n_fox, 1, ps * h), tot.reshape(n_phys, n_fox, 1, ps * h)


def _decode_kernel(pt_ref, q_ref, cq_ref, cn_ref, kn_ref, vn_ref, mask_ref, *refs, n_heads, n_q, dh, n_grp, n_steps):
    del pt_ref
    kps, vps = refs[:n_grp], refs[n_grp:2 * n_grp]
    sfs, tts = refs[2 * n_grp:3 * n_grp], refs[3 * n_grp:4 * n_grp]
    o_ref, m_s, l_s, acc_s, carry_s = refs[4 * n_grp:]
    p = pl.program_id(1)
    q = q_ref[0]
    cq = cq_ref[0]

    def update(us, vs):
        m_old = m_s[...]
        mu = functools.reduce(jnp.maximum, [jnp.max(u, axis=1, keepdims=True) for u in us])
        m_new = jnp.maximum(m_old, mu + cq)
        shift = m_new - cq
        alpha = jnp.exp(m_old - m_new)
        l = alpha * l_s[...]
        acc = alpha * acc_s[...]
        for u, v in zip(us, vs):
            pr = jnp.exp(u - shift)
            l = l + jnp.sum(pr, axis=1, keepdims=True)
            acc = acc + jnp.dot(pr.astype(_BF16), v, preferred_element_type=_F32)
        m_s[...] = m_new
        l_s[...] = l
        acc_s[...] = acc

    @pl.when(p == 0)
    def _():
        m_s[...] = jnp.full(m_s.shape, _NEG, _F32)
        l_s[...] = jnp.zeros(l_s.shape, _F32)
        acc_s[...] = jnp.zeros(acc_s.shape, _F32)
        carry_s[...] = jnp.zeros(carry_s.shape, _F32)
        s = lax.dot_general(q, kn_ref[0], _NT, preferred_element_type=_F32)
        row = lax.broadcasted_iota(jnp.int32, s.shape, 0)
        col = lax.broadcasted_iota(jnp.int32, s.shape, 1)
        valid = (lax.rem(col, n_heads) == row // n_q) & (col // n_heads <= lax.rem(row, n_q))
        update([jnp.where(valid, s - cn_ref[0], _NEG)], [vn_ref[0]])

    @pl.when(p > 0)
    def _():
        carry = carry_s[...]
        us, vs = [], []
        for g in range(n_grp):
            s = lax.dot_general(q, kps[g][...].astype(_BF16), _NT, preferred_element_type=_F32)
            us.append(s + ((sfs[g][...] + carry) + mask_ref[...]))
            carry = carry + tts[g][...]
            vs.append(vps[g][...].astype(_BF16))
        carry_s[...] = carry
        update(us, vs)

    @pl.when(p == n_steps - 1)
    def _():
        acc = acc_s[...] / l_s[...]
        for hh in range(n_heads):
            o_ref[0, :, hh * dh:(hh + 1) * dh] = acc[hh * n_q:(hh + 1) * n_q, :]


def _fox_decode(q, k_new, v_new, c_new, cache_k, cache_v, sfx, tot, page_table, layer):
    b, nq, h, dh = q.shape
    n_phys, n_fox, ps = cache_k.shape[:3]
    n_pages = page_table.shape[1]
    r = h * nq
    n_grp = next(g for g in (4, 2, 1) if n_pages % g == 0)
    n_steps = n_pages // n_grp + 1
    qs = (q * dh ** -0.5).transpose(0, 2, 1, 3).reshape(b, r, dh).astype(_BF16)
    cq = c_new.transpose(0, 2, 1).reshape(b, r, 1)
    row = lax.broadcasted_iota(jnp.int32, (r, ps * h), 0)
    col = lax.broadcasted_iota(jnp.int32, (r, ps * h), 1)
    mask = jnp.where(col % h == row // nq, 0.0, _NEG).astype(_F32)

    def page_spec(g, rows, last):
        def index(bi, p, pt):
            return pt[bi, n_pages - 1 - (jnp.maximum(p, 1) - 1) * n_grp - g], layer, 0, 0
        return pl.BlockSpec((None, None, rows, last), index)

    per_seq = lambda shape: pl.BlockSpec((1,) + shape, lambda bi, p, pt: (bi, 0, 0))
    grid_spec = pltpu.PrefetchScalarGridSpec(
        num_scalar_prefetch=1,
        grid=(b, n_steps),
        in_specs=[per_seq((r, dh)), per_seq((r, 1)), per_seq((1, nq * h)), per_seq((nq * h, dh)),
                  per_seq((nq * h, dh)), pl.BlockSpec((r, ps * h), lambda bi, p, pt: (0, 0))]
                 + [page_spec(g, ps * h, dh) for g in range(n_grp)] * 2
                 + [page_spec(g, 1, ps * h) for g in range(n_grp)] * 2,
        out_specs=per_seq((nq, h * dh)),
        scratch_shapes=[pltpu.VMEM((r, 1), _F32), pltpu.VMEM((r, 1), _F32), pltpu.VMEM((r, dh), _F32),
                        pltpu.VMEM((1, ps * h), _F32)],
    )
    kc = cache_k.reshape(n_phys, n_fox, ps * h, dh)
    vc = cache_v.reshape(n_phys, n_fox, ps * h, dh)
    out = pl.pallas_call(
        functools.partial(_decode_kernel, n_heads=h, n_q=nq, dh=dh, n_grp=n_grp, n_steps=n_steps),
        grid_spec=grid_spec,
        out_shape=jax.ShapeDtypeStruct((b, nq, h * dh), _F32),
        compiler_params=_params("arbitrary", "arbitrary"),
        name="fox_decode",
    )(page_table, qs, cq, c_new.reshape(b, 1, nq * h), k_new.reshape(b, nq * h, dh).astype(_BF16),
      v_new.reshape(b, nq * h, dh).astype(_BF16), mask, *([kc] * n_grp), *([vc] * n_grp),
      *([sfx] * n_grp), *([tot] * n_grp))
    return out.reshape(b * nq, h * dh).astype(_BF16)


def _first(accs, emn, en):
    return [accs[0]]


def _rwkv_layer(x, shift_prev, wkv_prev, v_first, j, i, W, Wb):
    b, l, d = x.shape
    h, n = wkv_prev.shape[1], wkv_prev.shape[2]
    xr, xw, xk, xv, xa, xg = _rwkv_mix(x, shift_prev, W['norm_mix'][i], W['rwkv_mu'][j])
    (r,) = _mm([xr], [Wb['rwkv_w_r'][j]], epilogue=_first, out_dtypes=[_F32], name="rwkv_r")
    decay = _lora(xw, W['rwkv_w1'][j], W['rwkv_w2'][j], act=jnp.tanh, extras_n=[W['rwkv_w0'][j]],
                  epilogue=lambda acc, en: jnp.exp(-jnp.exp(-_softplus(-(en[0] + acc)) - 0.5)), name="rwkv_decay")
    a = _lora(xa, W['rwkv_a1'][j], W['rwkv_a2'][j], act=lambda t: t, extras_n=[W['rwkv_a0'][j]],
              epilogue=lambda acc, en: _sigmoid(en[0] + acc), name="rwkv_a")
    g = _lora(xg, W['rwkv_g1'][j], W['rwkv_g2'][j], act=_sigmoid, epilogue=lambda acc, en: acc, name="rwkv_g")
    if j == 0:
        (v,) = _mm([xv], [Wb['rwkv_w_v'][j]], epilogue=_first, out_dtypes=[_F32], name="rwkv_v")
        v_first = v
    else:
        gate = _lora(xv, W['rwkv_v1'][j - 1], W['rwkv_v2'][j - 1], act=lambda t: t, extras_n=[W['rwkv_v0'][j - 1]],
                     epilogue=lambda acc, en: _sigmoid(en[0] + acc), name="rwkv_vgate")
        (v,) = _mm([xv], [Wb['rwkv_w_v'][j]], extras_mn=[v_first, gate], out_dtypes=[_F32], name="rwkv_v",
                   epilogue=lambda accs, emn, en: [accs[0] + (emn[0] - accs[0]) * emn[1]])
    kk_raw, k_mod = _mm([xk], [Wb['rwkv_w_k'][j]], extras_mn=[a], extras_n=[W['rwkv_k_k'][j], W['rwkv_k_a'][j]],
                        out_dtypes=[_F32, _F32], name="rwkv_k",
                        epilogue=lambda accs, emn, en: [accs[0] * en[0], accs[0] * (1.0 + (emn[0] - 1.0) * en[1])])
    p = -(-(b * h) // _LANES) * _LANES
    seqs = [_to_lanes(t, b, l, h, n, p) for t in (r, decay, k_mod, v, kk_raw, a)]
    s0 = jnp.pad(wkv_prev.astype(_F32).transpose(3, 2, 0, 1).reshape(n, n, b * h), ((0, 0), (0, 0), (0, p - b * h)))
    o, s_last = _rwkv_scan(seqs, _param_lanes(W['rwkv_r_k'][j].reshape(-1), b, h, n, p),
                           _param_lanes(W['rwkv_gn_w'][j], b, h, n, p), _param_lanes(W['rwkv_gn_b'][j], b, h, n, p), s0)
    o = _from_lanes(o, b, l, h, n)
    s_last = s_last[:, :, :b * h].reshape(n, n, b, h).transpose(2, 3, 1, 0)
    x2 = x.reshape(b * l, d)
    (x2,) = _mm([(o, g)], [Wb['rwkv_w_o'][j]], extras_mn=[x2], out_dtypes=[_F32], name="rwkv_out",
                epilogue=lambda accs, emn, en: [emn[0] + accs[0]])
    shift_out = _rmsnorm(x[:, -1, :], W['norm_mix'][i], _F32)
    return x2.reshape(b, l, d), shift_out, s_last, v_first


def _fox_layer(x, j, i, W, Wb, paged):
    b, l, d = x.shape
    h = W['fox_b_f'].shape[1]
    dh = d // h
    x2 = x.reshape(b * l, d)
    hn = _rmsnorm(x2, W['norm_mix'][i], _BF16)
    w_q, w_k, w_v, w_f = Wb['fox_in'][j]
    (q,) = _mm([hn], [w_q], epilogue=_first, out_dtypes=[_BF16 if paged is None else _F32], name="fox_q")
    k, kb = _mm([hn], [w_k], epilogue=lambda accs, emn, en: [accs[0], accs[0]], out_dtypes=[_F32, _BF16], name="fox_k")
    v, vb = _mm([hn], [w_v], epilogue=lambda accs, emn, en: [accs[0], accs[0]], out_dtypes=[_F32, _BF16], name="fox_v")
    bias_f = jnp.pad(W['fox_b_f'][j], (0, w_f.shape[1] - h))
    (logf_pad,) = _mm([hn], [w_f], extras_n=[bias_f], out_dtypes=[_F32], name="fox_logf",
                      epilogue=lambda accs, emn, en: [-_softplus(-(accs[0] + en[0]))])
    logf = logf_pad[:, :h].reshape(b, l, h)
    c_rows = _cumsum_rows(logf.transpose(0, 2, 1).reshape(b * h, l))
    if paged is None:
        o = _fox_attention(q.reshape(b, l, d), kb.reshape(b, l, d), vb.reshape(b, l, d),
                           c_rows.reshape(b * h, 1, l), b, l, h, dh)
        o = o.reshape(b * l, d)
    else:
        cache_k, cache_v, sfx, tot, page_table = paged
        c_new = c_rows.reshape(b, h, l).transpose(0, 2, 1)
        o = _fox_decode(q.reshape(b, l, h, dh), k.reshape(b, l, h, dh), v.reshape(b, l, h, dh), c_new,
                        cache_k, cache_v, sfx, tot, page_table, j)
    (x2,) = _mm([o], [Wb['fox_w_o'][j]], extras_mn=[x2], out_dtypes=[_F32], name="fox_out",
                epilogue=lambda accs, emn, en: [emn[0] + accs[0]])
    return x2.reshape(b, l, d), k.reshape(b, l, h, dh), v.reshape(b, l, h, dh), logf


def _ffn_ple(x, p_i, i, W, Wb):
    b, l, d = x.shape
    x2 = x.reshape(b * l, d)
    hn = _rmsnorm(x2, W['norm_ffn'][i], _BF16)
    (up,) = _mm([hn], [Wb['w_up'][i]], out_dtypes=[_BF16], name="ffn_up",
                epilogue=lambda accs, emn, en: [jnp.square(jnp.maximum(accs[0], 0.0))])
    (x2,) = _mm([up], [Wb['w_down'][i]], extras_mn=[x2], out_dtypes=[_F32], name="ffn_down",
                epilogue=lambda accs, emn, en: [emn[0] + accs[0]])
    hp = _rmsnorm(x2, W['norm_ple'][i], _BF16)
    pb = p_i.reshape(b * l, -1).astype(_BF16)
    (x2,) = _mm([hp, pb], [Wb['w_ple_gate'][i], Wb['w_ple'][i]], extras_mn=[x2], out_dtypes=[_F32], name="ple",
                epilogue=lambda accs, emn, en: [emn[0] + _sigmoid(accs[0]) * accs[1]])
    return x2.reshape(b, l, d)


def _trunk(x, p, shift_in, wkv_in, paged, W, Wb):
    depth = W['norm_mix'].shape[0]
    v_first = None
    shifts, wkvs, ks, vs, lfs = [], [], [], [], []
    for i in range(depth):
        j = i // 2
        if i % 2 == 0:
            x, sh, s_last, v_first = _rwkv_layer(x, shift_in[j], wkv_in[j], v_first, j, i, W, Wb)
            shifts.append(sh)
            wkvs.append(s_last)
        else:
            x, k, v, lf = _fox_layer(x, j, i, W, Wb, paged)
            ks.append(k)
            vs.append(v)
            lfs.append(lf)
        x = _ffn_ple(x, p[i], i, W, Wb)
    b, l, d = x.shape
    y = _rmsnorm(x.reshape(b * l, d), W['norm_out'], _F32).reshape(b, l, d)
    return (y, jnp.stack(shifts, 0), jnp.stack(wkvs, 0),
            jnp.stack(ks, 1), jnp.stack(vs, 1), jnp.stack(lfs, 1))


def kernel(x_prompt, x_sample, cache_k, cache_v, cache_logf, state_wkv, state_shift, page_table, p_prompt, p_sample, norm_mix, norm_ffn, norm_ple, norm_out, rwkv_mu, rwkv_w_r, rwkv_w_k, rwkv_w_v, rwkv_w_o, rwkv_w0, rwkv_w1, rwkv_w2, rwkv_a0, rwkv_a1, rwkv_a2, rwkv_v0, rwkv_v1, rwkv_v2, rwkv_g1, rwkv_g2, rwkv_k_k, rwkv_k_a, rwkv_r_k, rwkv_gn_w, rwkv_gn_b, fox_w_in, fox_b_f, fox_w_o, w_up, w_down, w_ple, w_ple_gate):
    W = {
        'norm_mix': norm_mix, 'norm_ffn': norm_ffn, 'norm_ple': norm_ple, 'norm_out': norm_out,
        'rwkv_mu': rwkv_mu, 'rwkv_w0': rwkv_w0, 'rwkv_w1': rwkv_w1, 'rwkv_w2': rwkv_w2,
        'rwkv_a0': rwkv_a0, 'rwkv_a1': rwkv_a1, 'rwkv_a2': rwkv_a2,
        'rwkv_v0': rwkv_v0, 'rwkv_v1': rwkv_v1, 'rwkv_v2': rwkv_v2,
        'rwkv_g1': rwkv_g1, 'rwkv_g2': rwkv_g2, 'rwkv_k_k': rwkv_k_k, 'rwkv_k_a': rwkv_k_a,
        'rwkv_r_k': rwkv_r_k, 'rwkv_gn_w': rwkv_gn_w, 'rwkv_gn_b': rwkv_gn_b, 'fox_b_f': fox_b_f,
    }
    d = x_prompt.shape[-1]
    n_fox_heads = fox_b_f.shape[1]
    f_pad = -(-n_fox_heads // _LANES) * _LANES - n_fox_heads
    bf = lambda w: w.astype(_BF16)
    Wb = {
        'rwkv_w_r': bf(rwkv_w_r), 'rwkv_w_k': bf(rwkv_w_k), 'rwkv_w_v': bf(rwkv_w_v), 'rwkv_w_o': bf(rwkv_w_o),
        'fox_in': [(bf(w[:, :d]), bf(w[:, d:2 * d]), bf(w[:, 2 * d:3 * d]),
                    bf(jnp.pad(w[:, 3 * d:], ((0, 0), (0, f_pad))))) for w in fox_w_in],
        'fox_w_o': bf(fox_w_o), 'w_up': bf(w_up), 'w_down': bf(w_down),
        'w_ple': bf(w_ple), 'w_ple_gate': bf(w_ple_gate),
    }
    n_rwkv, _, n_heads, n_dim, _ = state_wkv.shape
    b = x_prompt.shape[0]
    shift0 = jnp.zeros((n_rwkv, b, d), x_prompt.dtype)
    wkv0 = jnp.zeros((n_rwkv, b, n_heads, n_dim, n_dim), _F32)
    y_p, shift_p, wkv_p, k_p, v_p, lf_p = _trunk(x_prompt, p_prompt, shift0, wkv0, None, W, Wb)
    sfx, tot = _page_suffix(cache_logf)
    y_s, shift_s, wkv_s, k_s, v_s, lf_s = _trunk(
        x_sample, p_sample, state_shift, state_wkv, (cache_k, cache_v, sfx, tot, page_table), W, Wb)
    return (y_p, y_s, k_p, v_p, lf_p, k_s, v_s, lf_s, wkv_p, shift_p, wkv_s, shift_s)
```

```python
import functools

import jax
import jax.numpy as jnp
from jax import lax
from jax.experimental import pallas as pl
from jax.experimental.pallas import tpu as pltpu

_RMS_EPS = 1e-6
_GN_EPS = 64e-5
_KK_EPS = 1e-24
_NEG = -1e30
_LANES = 128
_SCAN_STEPS = 32
_ROW_PAD = 8
_VMEM_LIMIT = 52 * 1024 * 1024
_VMEM_BUDGET = 36 * 1024 * 1024
_F32 = jnp.float32
_BF16 = jnp.bfloat16
_NT = (((1,), (1,)), ((), ()))


def _params(*sem):
    return pltpu.CompilerParams(dimension_semantics=sem, vmem_limit_bytes=_VMEM_LIMIT)


def _divisor_tile(n, target, align):
    t = min(n, target)
    t -= t % align
    while t >= align:
        if n % t == 0:
            return t
        t -= align
    return n


def _sigmoid(x):
    return 1.0 / (1.0 + jnp.exp(-x))


def _softplus(x):
    return jnp.maximum(x, 0.0) + jnp.log(1.0 + jnp.exp(-jnp.abs(x)))


def _rms(x, g):
    return x * lax.rsqrt(jnp.mean(x * x, axis=-1, keepdims=True) + _RMS_EPS) * g


def _rmsnorm_kernel(x_ref, g_ref, o_ref):
    o_ref[...] = _rms(x_ref[...], g_ref[...]).astype(o_ref.dtype)


def _rmsnorm(x, g, out_dtype):
    t, d = x.shape
    tr = _divisor_tile(t, 512, 16)
    return pl.pallas_call(
        _rmsnorm_kernel,
        grid=(t // tr,),
        in_specs=[pl.BlockSpec((tr, d), lambda i: (i, 0)), pl.BlockSpec((1, d), lambda i: (0, 0))],
        out_specs=pl.BlockSpec((tr, d), lambda i: (i, 0)),
        out_shape=jax.ShapeDtypeStruct((t, d), out_dtype),
        compiler_params=_params("arbitrary"),
        name="rmsnorm",
    )(x, g.reshape(1, d))


def _mix_kernel(x_ref, xp_ref, sh_ref, g_ref, mu_ref, *o_refs):
    g = g_ref[...]
    xn = _rms(x_ref[0], g)
    prev = _rms(xp_ref[0, 7:8, :], g)
    prev = jnp.where(pl.program_id(1) == 0, sh_ref[0], prev)
    row = lax.broadcasted_iota(jnp.int32, xn.shape, 0)
    xprev = jnp.where(row == 0, prev, pltpu.roll(xn, 1, axis=0))
    xx = xprev - xn
    for c, o_ref in enumerate(o_refs):
        o_ref[0] = (xn + xx * mu_ref[c:c + 1, :]).astype(o_ref.dtype)


def _rwkv_mix(x, shift_prev, g, mu):
    b, l, d = x.shape
    tr = _divisor_tile(l, 256, 8)
    nb8 = tr // 8
    outs = pl.pallas_call(
        _mix_kernel,
        grid=(b, l // tr),
        in_specs=[
            pl.BlockSpec((1, tr, d), lambda bi, i: (bi, i, 0)),
            pl.BlockSpec((1, 8, d), lambda bi, i: (bi, jnp.maximum(i * nb8 - 1, 0), 0)),
            pl.BlockSpec((1, 1, d), lambda bi, i: (bi, 0, 0)),
            pl.BlockSpec((1, d), lambda bi, i: (0, 0)),
            pl.BlockSpec((6, d), lambda bi, i: (0, 0)),
        ],
        out_specs=[pl.BlockSpec((1, tr, d), lambda bi, i: (bi, i, 0))] * 6,
        out_shape=[jax.ShapeDtypeStruct((b, l, d), _BF16)] * 6,
        compiler_params=_params("arbitrary", "arbitrary"),
        name="rwkv_mix",
    )(x, x, shift_prev.reshape(b, 1, d), g.reshape(1, d), mu)
    return [o.reshape(b * l, d) for o in outs]


def _mm_kernel(*refs, n_w, n_emn, n_en, epilogue):
    xs = [refs[i][...] for i in range(n_w)]
    ws = [refs[n_w + i][...] for i in range(n_w)]
    pos = 2 * n_w
    emn = [refs[pos + i][...] for i in range(n_emn)]
    pos += n_emn
    en = [refs[pos + i][...] for i in range(n_en)]
    pos += n_en
    accs = [jnp.dot(x, w, preferred_element_type=_F32) for x, w in zip(xs, ws)]
    outs = epilogue(accs, emn, en)
    for o_ref, o in zip(refs[pos:], outs):
        o_ref[...] = o.astype(o_ref.dtype)


def _mm_tiles(m, n, x_row_bytes, k_total, n_emn, out_bytes):
    best = None
    for tm in (1024, 512, 256, 128, 64, 32, 16, 8):
        if tm > m or m % tm:
            continue
        for tn in (1024, 512, 256, 128):
            if tn > n or n % tn:
                continue
            need = 2 * (tm * x_row_bytes + k_total * tn * 2 + tm * tn * (4 * n_emn + out_bytes)) + tm * tn * 4
            if need > _VMEM_BUDGET:
                continue
            score = (tm * tn / (tm + tn), tm)
            if best is None or score > best[0]:
                best = (score, tm, tn)
    assert best is not None, (m, n)
    return best[1], best[2]


def _mm(xs, ws, *, epilogue, out_dtypes, extras_mn=(), extras_n=(), name="mm"):
    m = xs[0].shape[0]
    n = ws[0].shape[1]
    x_row_bytes = sum(a.shape[1] * a.dtype.itemsize for a in xs)
    k_total = sum(w.shape[0] for w in ws)
    out_bytes = sum(jnp.dtype(dt).itemsize for dt in out_dtypes)
    tm, tn = _mm_tiles(m, n, x_row_bytes, k_total, len(extras_mn), out_bytes)
    in_specs = [pl.BlockSpec((tm, a.shape[1]), lambda i, j: (i, 0)) for a in xs]
    in_specs += [pl.BlockSpec((w.shape[0], tn), lambda i, j: (0, j)) for w in ws]
    in_specs += [pl.BlockSpec((tm, tn), lambda i, j: (i, j)) for _ in extras_mn]
    in_specs += [pl.BlockSpec((1, tn), lambda i, j: (0, j)) for _ in extras_n]
    kern = functools.partial(_mm_kernel, n_w=len(ws), n_emn=len(extras_mn), n_en=len(extras_n), epilogue=epilogue)
    outs = pl.pallas_call(
        kern,
        grid=(m // tm, n // tn),
        in_specs=in_specs,
        out_specs=[pl.BlockSpec((tm, tn), lambda i, j: (i, j)) for _ in out_dtypes],
        out_shape=[jax.ShapeDtypeStruct((m, n), dt) for dt in out_dtypes],
        compiler_params=_params("arbitrary", "arbitrary"),
        name=name,
    )(*xs, *ws, *extras_mn, *[e.reshape(1, n) for e in extras_n])
    return outs


def _lora_kernel(x_ref, w1_ref, w2_ref, *refs, act, n_en, epilogue):
    h = jnp.dot(x_ref[...], w1_ref[...], preferred_element_type=_F32)
    acc = jnp.dot(act(h).astype(_BF16), w2_ref[...], preferred_element_type=_F32)
    en = [refs[i][...] for i in range(n_en)]
    refs[n_en][...] = epilogue(acc, en).astype(refs[n_en].dtype)


def _lora(x, w1, w2, *, act, epilogue, extras_n=(), name="lora"):
    m, k = x.shape
    r = w1.shape[1]
    n = w2.shape[1]
    rp = -(-r // _LANES) * _LANES
    w1 = jnp.pad(w1, ((0, 0), (0, rp - r))).astype(_BF16)
    w2 = jnp.pad(w2, ((0, rp - r), (0, 0))).astype(_BF16)
    tm = _divisor_tile(m, 512, 16)
    kern = functools.partial(_lora_kernel, act=act, n_en=len(extras_n), epilogue=epilogue)
    return pl.pallas_call(
        kern,
        grid=(m // tm,),
        in_specs=[pl.BlockSpec((tm, k), lambda i: (i, 0)),
                  pl.BlockSpec((k, rp), lambda i: (0, 0)),
                  pl.BlockSpec((rp, n), lambda i: (0, 0))]
                 + [pl.BlockSpec((1, n), lambda i: (0, 0)) for _ in extras_n],
        out_specs=pl.BlockSpec((tm, n), lambda i: (i, 0)),
        out_shape=jax.ShapeDtypeStruct((m, n), _F32),
        compiler_params=_params("arbitrary"),
        name=name,
    )(x, w1, w2, *[e.reshape(1, n) for e in extras_n])


def _scan_kernel(r_ref, w_ref, k_ref, v_ref, a_ref, kkp_ref, kap_ref, rk_ref, gw_ref, gb_ref, s0_ref,
                 o_ref, s_ref, kk_s, kka_s, km_s, *, tb, n):
    @pl.when(pl.program_id(1) == 0)
    def _():
        s_ref[...] = s0_ref[...]

    def step(t, carry):
        k_t = k_ref[t, 0:n, :]
        a_t = a_ref[t, 0:n, :]
        kk = k_t * kkp_ref[...]
        kk = kk * lax.rsqrt(jnp.maximum(jnp.sum(kk * kk, axis=0, keepdims=True), _KK_EPS))
        km = k_t * (1.0 + (a_t - 1.0) * kap_ref[...])
        kk_s[...] = kk
        kka_s[...] = kk * a_t
        km_s[...] = km
        v_t = v_ref[t, 0:n, :]
        sk = [jnp.zeros_like(v_t), jnp.zeros_like(v_t)]
        for j in range(n):
            sk[j % 2] = sk[j % 2] + s_ref[j] * kk_s[j:j + 1, :]
        sk = sk[0] + sk[1]
        o = [jnp.zeros_like(v_t), jnp.zeros_like(v_t)]
        for j in range(n):
            s_new = (s_ref[j] * w_ref[t, j:j + 1, :]
                     + (v_t * km_s[j:j + 1, :] - sk * kka_s[j:j + 1, :]))
            s_ref[j] = s_new
            o[j % 2] = o[j % 2] + s_new * r_ref[t, j:j + 1, :]
        o = o[0] + o[1]
        mean = jnp.mean(o, axis=0, keepdims=True)
        var = jnp.mean(jnp.square(o - mean), axis=0, keepdims=True)
        o = (o - mean) * lax.rsqrt(var + _GN_EPS) * gw_ref[...] + gb_ref[...]
        bonus = jnp.sum(r_ref[t, 0:n, :] * km * rk_ref[...], axis=0, keepdims=True) * v_t
        o_ref[t, 0:n, :] = o + bonus
        o_ref[t, n:, :] = jnp.zeros((o_ref.shape[1] - n, _LANES), _F32)
        return carry

    lax.fori_loop(0, tb, step, 0)


def _rwkv_scan(seqs, params, s0, n):
    l, rows, p = seqs[0].shape
    tb = _divisor_tile(l, _SCAN_STEPS, 1)
    seq_spec = pl.BlockSpec((tb, rows, _LANES), lambda c, i: (i, 0, c))
    par_spec = pl.BlockSpec((n, _LANES), lambda c, i: (0, c))
    st_spec = pl.BlockSpec((n, n, _LANES), lambda c, i: (0, 0, c))
    return pl.pallas_call(
        functools.partial(_scan_kernel, tb=tb, n=n),
        grid=(p // _LANES, l // tb),
        in_specs=[seq_spec] * 5 + [par_spec] * 5 + [st_spec],
        out_specs=[seq_spec, st_spec],
        out_shape=[jax.ShapeDtypeStruct((l, rows, p), _F32), jax.ShapeDtypeStruct((n, n, p), _F32)],
        scratch_shapes=[pltpu.VMEM((n, _LANES), _F32)] * 3,
        compiler_params=_params("arbitrary", "arbitrary"),
        name="rwkv_scan",
    )(*seqs, *params, s0)


def _lanes_in_pallas(b, l, h, n):
    return l % _LANES == 0 and _LANES % h == 0 and b % (_LANES // h) == 0 and _LANES % n == 0


def _to_lanes_kernel(x_ref, o_ref, y_s, *, n):
    g, steps, d = x_ref.shape
    tiles = d // _LANES
    pitch = o_ref.shape[0] // steps
    for b in range(g):
        for c in range(tiles):
            r0 = (b * tiles + c) * _LANES
            y_s[r0:r0 + _LANES, :] = x_ref[b, :, c * _LANES:(c + 1) * _LANES].T
    for j in range(n):
        o_ref[pl.ds(j, steps, stride=pitch), :] = y_s[pl.ds(j, _LANES, stride=n), :].T
    for j in range(n, pitch):
        o_ref[pl.ds(j, steps, stride=pitch), :] = jnp.zeros((steps, _LANES), _F32)


def _from_lanes_kernel(o_ref, g_ref, out_ref, y_s, *, n):
    g, steps, d = g_ref.shape
    tiles = d // _LANES
    pitch = o_ref.shape[0] // steps
    for j in range(n):
        y_s[pl.ds(j, _LANES, stride=n), :] = o_ref[pl.ds(j, steps, stride=pitch), :].T
    for b in range(g):
        for c in range(tiles):
            r0 = (b * tiles + c) * _LANES
            cols = slice(c * _LANES, (c + 1) * _LANES)
            out_ref[b, :, cols] = (y_s[r0:r0 + _LANES, :].T * g_ref[b, :, cols]).astype(out_ref.dtype)


def _to_lanes(x, b, l, h, n, p):
    d = h * n
    rows = n + _ROW_PAD
    if not _lanes_in_pallas(b, l, h, n):
        y = x.reshape(b, l, h, n).transpose(1, 3, 0, 2).reshape(l, n, b * h)
        return jnp.pad(y, ((0, 0), (0, _ROW_PAD), (0, p - b * h)))
    g = _LANES // h
    out = pl.pallas_call(
        functools.partial(_to_lanes_kernel, n=n),
        grid=(p // _LANES, l // _LANES),
        in_specs=[pl.BlockSpec((g, _LANES, d), lambda c, i: (c, i, 0))],
        out_specs=pl.BlockSpec((_LANES * rows, _LANES), lambda c, i: (i, c)),
        out_shape=jax.ShapeDtypeStruct((l * rows, p), _F32),
        scratch_shapes=[pltpu.VMEM((g * d, _LANES), _F32)],
        compiler_params=_params("arbitrary", "arbitrary"),
        name="to_lanes",
    )(x.reshape(b, l, d))
    return out.reshape(l, rows, p)


def _from_lanes(o, gate, b, l, h, n):
    d = h * n
    rows, p = o.shape[1:]
    if not _lanes_in_pallas(b, l, h, n):
        y = o[:, :n, :b * h].reshape(l, n, b, h).transpose(2, 0, 3, 1).reshape(b * l, d)
        return (y * gate).astype(_BF16)
    g = _LANES // h
    out = pl.pallas_call(
        functools.partial(_from_lanes_kernel, n=n),
        grid=(p // _LANES, l // _LANES),
        in_specs=[pl.BlockSpec((_LANES * rows, _LANES), lambda c, i: (i, c)),
                  pl.BlockSpec((g, _LANES, d), lambda c, i: (c, i, 0))],
        out_specs=pl.BlockSpec((g, _LANES, d), lambda c, i: (c, i, 0)),
        out_shape=jax.ShapeDtypeStruct((b, l, d), _BF16),
        scratch_shapes=[pltpu.VMEM((g * d, _LANES), _F32)],
        compiler_params=_params("arbitrary", "arbitrary"),
        name="from_lanes",
    )(o.reshape(l * rows, p), gate.reshape(b, l, d))
    return out.reshape(b * l, d)


def _param_lanes(w, b, h, n, p):
    y = jnp.broadcast_to(w.reshape(h, n).T[:, None, :], (n, b, h)).reshape(n, b * h)
    return jnp.pad(y, ((0, 0), (0, p - b * h)))


def _cumsum_kernel(x_ref, o_ref):
    x = x_ref[...]
    length = x.shape[1]
    col = lax.broadcasted_iota(jnp.int32, x.shape, 1)
    if length % _LANES == 0:
        s = 1
        while s < length:
            x = x + jnp.where(col >= s, pltpu.roll(x, s, axis=1), 0.0)
            s *= 2
        o_ref[...] = x
    else:
        acc = jnp.zeros_like(x)
        for s in range(length):
            acc = acc + jnp.where(col >= s, x[:, s:s + 1], 0.0)
        o_ref[...] = acc


def _cumsum_rows(x):
    return pl.pallas_call(
        _cumsum_kernel,
        out_shape=jax.ShapeDtypeStruct(x.shape, _F32),
        compiler_params=pltpu.CompilerParams(vmem_limit_bytes=_VMEM_LIMIT),
        name="logf_cumsum",
    )(x)


def _fox_kernel(q_ref, k_ref, v_ref, c_ref, o_ref, *, tq, dh, n_grp, scale):
    i = pl.program_id(2)
    row = lax.broadcasted_iota(jnp.int32, (tq, tq), 0)
    col = lax.broadcasted_iota(jnp.int32, (tq, tq), 1)
    q0 = pl.multiple_of(i * tq, tq)
    qs = [q_ref[0, :, g * dh:(g + 1) * dh] for g in range(n_grp)]
    cqs = [jnp.sum(jnp.where(row == col, c_ref[g, :, pl.ds(q0, tq)], 0.0), axis=1, keepdims=True)
           for g in range(n_grp)]

    def block(j, carry, masked):
        k0 = pl.multiple_of(j * tq, tq)
        out = []
        for g in range(n_grp):
            m, l, acc = carry[g]
            lanes = slice(g * dh, (g + 1) * dh)
            s = lax.dot_general(qs[g], k_ref[0, pl.ds(k0, tq), lanes], _NT, preferred_element_type=_F32)
            u = s * scale - c_ref[g, :, pl.ds(k0, tq)]
            if masked:
                u = jnp.where(col <= row, u, _NEG)
            m_new = jnp.maximum(m, jnp.max(u, axis=1, keepdims=True) + cqs[g])
            alpha = jnp.exp(m - m_new)
            p = jnp.exp(u - (m_new - cqs[g]))
            l = alpha * l + jnp.sum(p, axis=1, keepdims=True)
            acc = alpha * acc + jnp.dot(p.astype(_BF16), v_ref[0, pl.ds(k0, tq), lanes],
                                        preferred_element_type=_F32)
            out.append((m_new, l, acc))
        return tuple(out)

    init = tuple((jnp.full((tq, 1), _NEG, _F32), jnp.zeros((tq, 1), _F32), jnp.zeros((tq, dh), _F32))
                 for _ in range(n_grp))
    carry = lax.fori_loop(0, i, lambda j, c: block(j, c, False), init)
    carry = block(i, carry, True)
    for g in range(n_grp):
        _, l, acc = carry[g]
        o_ref[0, :, g * dh:(g + 1) * dh] = (acc / l).astype(o_ref.dtype)


def _fox_attention(q, k, v, c_rows, b, l, h, dh):
    tq = _divisor_tile(l, 512, _LANES)
    n_grp = 2 if h % 2 == 0 else 1
    qo_spec = pl.BlockSpec((1, tq, n_grp * dh), lambda bi, hi, i: (bi, i, hi))
    kv_spec = pl.BlockSpec((1, l, n_grp * dh), lambda bi, hi, i: (bi, 0, hi))
    c_spec = pl.BlockSpec((n_grp, 1, l), lambda bi, hi, i: (bi * (h // n_grp) + hi, 0, 0))
    return pl.pallas_call(
        functools.partial(_fox_kernel, tq=tq, dh=dh, n_grp=n_grp, scale=dh ** -0.5),
        grid=(b, h // n_grp, l // tq),
        in_specs=[qo_spec, kv_spec, kv_spec, c_spec],
        out_specs=qo_spec,
        out_shape=jax.ShapeDtypeStruct((b, l, h * dh), _BF16),
        compiler_params=_params("arbitrary", "arbitrary", "arbitrary"),
        name="fox_attention",
    )(q, k, v, c_rows)


def _suffix_kernel(x_ref, sfx_ref, tot_ref, *, stride):
    x = x_ref[...]
    n = x.shape[1]
    col = lax.broadcasted_iota(jnp.int32, x.shape, 1)
    inc = x
    tot = x
    s = stride
    while s < n:
        inc = inc + jnp.where(col + s < n, pltpu.roll(inc, n - s, axis=1), 0.0)
        tot = tot + pltpu.roll(tot, n - s, axis=1)
        s *= 2
    sfx_ref[...] = inc - x
    tot_ref[...] = tot


def _page_suffix(cache_logf):
    n_phys, n_fox, ps, h = cache_logf.shape
    assert ps & (ps - 1) == 0 and (ps * h) % _LANES == 0, (ps, h)
    rows = n_phys * n_fox
    tr = _divisor_tile(rows, 256, 8)
    spec = pl.BlockSpec((tr, ps * h), lambda i: (i, 0))
    sfx, tot = pl.pallas_call(
        functools.partial(_suffix_kernel, stride=h),
        grid=(rows // tr,),
        in_specs=[spec],
        out_specs=[spec, spec],
        out_shape=[jax.ShapeDtypeStruct((rows, ps * h), _F32)] * 2,
        compiler_params=_params("arbitrary"),
        name="page_suffix",
    )(cache_logf.reshape(rows, ps * h))
    return sfx.reshape(n_phys, n_fox, 1, ps * h), tot.reshape(n_phys, n_fox, 1, ps * h)


def _decode_kernel(pt_ref, q_ref, cq_ref, cn_ref, kn_ref, vn_ref, mask_ref, *refs, n_heads, n_q, dh, n_grp, n_steps):
    del pt_ref
    kps, vps = refs[:n_grp], refs[n_grp:2 * n_grp]
    sfs, tts = refs[2 * n_grp:3 * n_grp], refs[3 * n_grp:4 * n_grp]
    o_ref, m_s, l_s, acc_s, carry_s = refs[4 * n_grp:]
    p = pl.program_id(1)
    q = q_ref[0]
    cq = cq_ref[0]

    def update(us, vs):
        m_old = m_s[...]
        mu = functools.reduce(jnp.maximum, [jnp.max(u, axis=1, keepdims=True) for u in us])
        m_new = jnp.maximum(m_old, mu + cq)
        shift = m_new - cq
        alpha = jnp.exp(m_old - m_new)
        l = alpha * l_s[...]
        acc = alpha * acc_s[...]
        for u, v in zip(us, vs):
            pr = jnp.exp(u - shift)
            l = l + jnp.sum(pr, axis=1, keepdims=True)
            acc = acc + jnp.dot(pr.astype(_BF16), v, preferred_element_type=_F32)
        m_s[...] = m_new
        l_s[...] = l
        acc_s[...] = acc

    @pl.when(p == 0)
    def _():
        m_s[...] = jnp.full(m_s.shape, _NEG, _F32)
        l_s[...] = jnp.zeros(l_s.shape, _F32)
        acc_s[...] = jnp.zeros(acc_s.shape, _F32)
        carry_s[...] = jnp.zeros(carry_s.shape, _F32)
        s = lax.dot_general(q, kn_ref[0], _NT, preferred_element_type=_F32)
        row = lax.broadcasted_iota(jnp.int32, s.shape, 0)
        col = lax.broadcasted_iota(jnp.int32, s.shape, 1)
        valid = (lax.rem(col, n_heads) == row // n_q) & (col // n_heads <= lax.rem(row, n_q))
        update([jnp.where(valid, s - cn_ref[0], _NEG)], [vn_ref[0]])

    @pl.when(p > 0)
    def _():
        carry = carry_s[...]
        us, vs = [], []
        for g in range(n_grp):
            s = lax.dot_general(q, kps[g][...].astype(_BF16), _NT, preferred_element_type=_F32)
            us.append(s + ((sfs[g][...] + carry) + mask_ref[...]))
            carry = carry + tts[g][...]
            vs.append(vps[g][...].astype(_BF16))
        carry_s[...] = carry
        update(us, vs)

    @pl.when(p == n_steps - 1)
    def _():
        acc = acc_s[...] / l_s[...]
        for hh in range(n_heads):
            o_ref[0, :, hh * dh:(hh + 1) * dh] = acc[hh * n_q:(hh + 1) * n_q, :]


def _fox_decode(q, k_new, v_new, c_new, cache_k, cache_v, sfx, tot, page_table, layer):
    b, nq, h, dh = q.shape
    n_phys, n_fox, ps = cache_k.shape[:3]
    n_pages = page_table.shape[1]
    r = h * nq
    n_grp = next(g for g in (4, 2, 1) if n_pages % g == 0)
    n_steps = n_pages // n_grp + 1
    qs = (q * dh ** -0.5).transpose(0, 2, 1, 3).reshape(b, r, dh).astype(_BF16)
    cq = c_new.transpose(0, 2, 1).reshape(b, r, 1)
    row = lax.broadcasted_iota(jnp.int32, (r, ps * h), 0)
    col = lax.broadcasted_iota(jnp.int32, (r, ps * h), 1)
    mask = jnp.where(col % h == row // nq, 0.0, _NEG).astype(_F32)

    def page_spec(g, rows, last):
        def index(bi, p, pt):
            return pt[bi, n_pages - 1 - (jnp.maximum(p, 1) - 1) * n_grp - g], layer, 0, 0
        return pl.BlockSpec((None, None, rows, last), index)

    per_seq = lambda shape: pl.BlockSpec((1,) + shape, lambda bi, p, pt: (bi, 0, 0))
    grid_spec = pltpu.PrefetchScalarGridSpec(
        num_scalar_prefetch=1,
        grid=(b, n_steps),
        in_specs=[per_seq((r, dh)), per_seq((r, 1)), per_seq((1, nq * h)), per_seq((nq * h, dh)),
                  per_seq((nq * h, dh)), pl.BlockSpec((r, ps * h), lambda bi, p, pt: (0, 0))]
                 + [page_spec(g, ps * h, dh) for g in range(n_grp)] * 2
                 + [page_spec(g, 1, ps * h) for g in range(n_grp)] * 2,
        out_specs=per_seq((nq, h * dh)),
        scratch_shapes=[pltpu.VMEM((r, 1), _F32), pltpu.VMEM((r, 1), _F32), pltpu.VMEM((r, dh), _F32),
                        pltpu.VMEM((1, ps * h), _F32)],
    )
    kc = cache_k.reshape(n_phys, n_fox, ps * h, dh)
    vc = cache_v.reshape(n_phys, n_fox, ps * h, dh)
    out = pl.pallas_call(
        functools.partial(_decode_kernel, n_heads=h, n_q=nq, dh=dh, n_grp=n_grp, n_steps=n_steps),
        grid_spec=grid_spec,
        out_shape=jax.ShapeDtypeStruct((b, nq, h * dh), _F32),
        compiler_params=_params("arbitrary", "arbitrary"),
        name="fox_decode",
    )(page_table, qs, cq, c_new.reshape(b, 1, nq * h), k_new.reshape(b, nq * h, dh).astype(_BF16),
      v_new.reshape(b, nq * h, dh).astype(_BF16), mask, *([kc] * n_grp), *([vc] * n_grp),
      *([sfx] * n_grp), *([tot] * n_grp))
    return out.reshape(b * nq, h * dh).astype(_BF16)


def _first(accs, emn, en):
    return [accs[0]]


def _rwkv_layer(x, shift_prev, wkv_prev, v_first, j, i, W, Wb):
    b, l, d = x.shape
    h, n = wkv_prev.shape[1], wkv_prev.shape[2]
    xr, xw, xk, xv, xa, xg = _rwkv_mix(x, shift_prev, W['norm_mix'][i], W['rwkv_mu'][j])
    (r,) = _mm([xr], [Wb['rwkv_w_r'][j]], epilogue=_first, out_dtypes=[_F32], name="rwkv_r")
    decay = _lora(xw, W['rwkv_w1'][j], W['rwkv_w2'][j], act=jnp.tanh, extras_n=[W['rwkv_w0'][j]],
                  epilogue=lambda acc, en: jnp.exp(-jnp.exp(-_softplus(-(en[0] + acc)) - 0.5)), name="rwkv_decay")
    a = _lora(xa, W['rwkv_a1'][j], W['rwkv_a2'][j], act=lambda t: t, extras_n=[W['rwkv_a0'][j]],
              epilogue=lambda acc, en: _sigmoid(en[0] + acc), name="rwkv_a")
    g = _lora(xg, W['rwkv_g1'][j], W['rwkv_g2'][j], act=_sigmoid, epilogue=lambda acc, en: acc, name="rwkv_g")
    if j == 0:
        (v,) = _mm([xv], [Wb['rwkv_w_v'][j]], epilogue=_first, out_dtypes=[_F32], name="rwkv_v")
        v_first = v
    else:
        gate = _lora(xv, W['rwkv_v1'][j - 1], W['rwkv_v2'][j - 1], act=lambda t: t, extras_n=[W['rwkv_v0'][j - 1]],
                     epilogue=lambda acc, en: _sigmoid(en[0] + acc), name="rwkv_vgate")
        (v,) = _mm([xv], [Wb['rwkv_w_v'][j]], extras_mn=[v_first, gate], out_dtypes=[_F32], name="rwkv_v",
                   epilogue=lambda accs, emn, en: [accs[0] + (emn[0] - accs[0]) * emn[1]])
    (k,) = _mm([xk], [Wb['rwkv_w_k'][j]], epilogue=_first, out_dtypes=[_F32], name="rwkv_k")
    p = -(-(b * h) // _LANES) * _LANES
    seqs = [_to_lanes(t, b, l, h, n, p) for t in (r, decay, k, v, a)]
    params = [_param_lanes(w.reshape(-1), b, h, n, p) for w in
              (W['rwkv_k_k'][j], W['rwkv_k_a'][j], W['rwkv_r_k'][j], W['rwkv_gn_w'][j], W['rwkv_gn_b'][j])]
    s0 = jnp.pad(wkv_prev.astype(_F32).transpose(3, 2, 0, 1).reshape(n, n, b * h), ((0, 0), (0, 0), (0, p - b * h)))
    o, s_last = _rwkv_scan(seqs, params, s0, n)
    og = _from_lanes(o, g, b, l, h, n)
    s_last = s_last[:, :, :b * h].reshape(n, n, b, h).transpose(2, 3, 1, 0)
    x2 = x.reshape(b * l, d)
    (x2,) = _mm([og], [Wb['rwkv_w_o'][j]], extras_mn=[x2], out_dtypes=[_F32], name="rwkv_out",
                epilogue=lambda accs, emn, en: [emn[0] + accs[0]])
    shift_out = _rmsnorm(x[:, -1, :], W['norm_mix'][i], _F32)
    return x2.reshape(b, l, d), shift_out, s_last, v_first


def _fox_layer(x, j, i, W, Wb, paged):
    b, l, d = x.shape
    h = W['fox_b_f'].shape[1]
    dh = d // h
    x2 = x.reshape(b * l, d)
    hn = _rmsnorm(x2, W['norm_mix'][i], _BF16)
    w_q, w_k, w_v, w_f = Wb['fox_in'][j]
    (q,) = _mm([hn], [w_q], epilogue=_first, out_dtypes=[_BF16 if paged is None else _F32], name="fox_q")
    k, kb = _mm([hn], [w_k], epilogue=lambda accs, emn, en: [accs[0], accs[0]], out_dtypes=[_F32, _BF16], name="fox_k")
    v, vb = _mm([hn], [w_v], epilogue=lambda accs, emn, en: [accs[0], accs[0]], out_dtypes=[_F32, _BF16], name="fox_v")
    bias_f = jnp.pad(W['fox_b_f'][j], (0, w_f.shape[1] - h))
    (logf_pad,) = _mm([hn], [w_f], extras_n=[bias_f], out_dtypes=[_F32], name="fox_logf",
                      epilogue=lambda accs, emn, en: [-_softplus(-(accs[0] + en[0]))])
    logf = logf_pad[:, :h].reshape(b, l, h)
    c_rows = _cumsum_rows(logf.transpose(0, 2, 1).reshape(b * h, l))
    if paged is None:
        o = _fox_attention(q.reshape(b, l, d), kb.reshape(b, l, d), vb.reshape(b, l, d),
                           c_rows.reshape(b * h, 1, l), b, l, h, dh)
        o = o.reshape(b * l, d)
    else:
        cache_k, cache_v, sfx, tot, page_table = paged
        c_new = c_rows.reshape(b, h, l).transpose(0, 2, 1)
        o = _fox_decode(q.reshape(b, l, h, dh), k.reshape(b, l, h, dh), v.reshape(b, l, h, dh), c_new,
                        cache_k, cache_v, sfx, tot, page_table, j)
    (x2,) = _mm([o], [Wb['fox_w_o'][j]], extras_mn=[x2], out_dtypes=[_F32], name="fox_out",
                epilogue=lambda accs, emn, en: [emn[0] + accs[0]])
    return x2.reshape(b, l, d), k.reshape(b, l, h, dh), v.reshape(b, l, h, dh), logf


def _ffn_ple(x, p_i, i, W, Wb):
    b, l, d = x.shape
    x2 = x.reshape(b * l, d)
    hn = _rmsnorm(x2, W['norm_ffn'][i], _BF16)
    (up,) = _mm([hn], [Wb['w_up'][i]], out_dtypes=[_BF16], name="ffn_up",
                epilogue=lambda accs, emn, en: [jnp.square(jnp.maximum(accs[0], 0.0))])
    (x2,) = _mm([up], [Wb['w_down'][i]], extras_mn=[x2], out_dtypes=[_F32], name="ffn_down",
                epilogue=lambda accs, emn, en: [emn[0] + accs[0]])
    hp = _rmsnorm(x2, W['norm_ple'][i], _BF16)
    pb = p_i.reshape(b * l, -1).astype(_BF16)
    (x2,) = _mm([hp, pb], [Wb['w_ple_gate'][i], Wb['w_ple'][i]], extras_mn=[x2], out_dtypes=[_F32], name="ple",
                epilogue=lambda accs, emn, en: [emn[0] + _sigmoid(accs[0]) * accs[1]])
    return x2.reshape(b, l, d)


def _trunk(x, p, shift_in, wkv_in, paged, W, Wb):
    depth = W['norm_mix'].shape[0]
    v_first = None
    shifts, wkvs, ks, vs, lfs = [], [], [], [], []
    for i in range(depth):
        j = i // 2
        if i % 2 == 0:
            x, sh, s_last, v_first = _rwkv_layer(x, shift_in[j], wkv_in[j], v_first, j, i, W, Wb)
            shifts.append(sh)
            wkvs.append(s_last)
        else:
            x, k, v, lf = _fox_layer(x, j, i, W, Wb, paged)
            ks.append(k)
            vs.append(v)
            lfs.append(lf)
        x = _ffn_ple(x, p[i], i, W, Wb)
    b, l, d = x.shape
    y = _rmsnorm(x.reshape(b * l, d), W['norm_out'], _F32).reshape(b, l, d)
    return (y, jnp.stack(shifts, 0), jnp.stack(wkvs, 0),
            jnp.stack(ks, 1), jnp.stack(vs, 1), jnp.stack(lfs, 1))


def kernel(x_prompt, x_sample, cache_k, cache_v, cache_logf, state_wkv, state_shift, page_table, p_prompt, p_sample, norm_mix, norm_ffn, norm_ple, norm_out, rwkv_mu, rwkv_w_r, rwkv_w_k, rwkv_w_v, rwkv_w_o, rwkv_w0, rwkv_w1, rwkv_w2, rwkv_a0, rwkv_a1, rwkv_a2, rwkv_v0, rwkv_v1, rwkv_v2, rwkv_g1, rwkv_g2, rwkv_k_k, rwkv_k_a, rwkv_r_k, rwkv_gn_w, rwkv_gn_b, fox_w_in, fox_b_f, fox_w_o, w_up, w_down, w_ple, w_ple_gate):
    W = {
        'norm_mix': norm_mix, 'norm_ffn': norm_ffn, 'norm_ple': norm_ple, 'norm_out': norm_out,
        'rwkv_mu': rwkv_mu, 'rwkv_w0': rwkv_w0, 'rwkv_w1': rwkv_w1, 'rwkv_w2': rwkv_w2,
        'rwkv_a0': rwkv_a0, 'rwkv_a1': rwkv_a1, 'rwkv_a2': rwkv_a2,
        'rwkv_v0': rwkv_v0, 'rwkv_v1': rwkv_v1, 'rwkv_v2': rwkv_v2,
        'rwkv_g1': rwkv_g1, 'rwkv_g2': rwkv_g2, 'rwkv_k_k': rwkv_k_k, 'rwkv_k_a': rwkv_k_a,
        'rwkv_r_k': rwkv_r_k, 'rwkv_gn_w': rwkv_gn_w, 'rwkv_gn_b': rwkv_gn_b, 'fox_b_f': fox_b_f,
    }
    d = x_prompt.shape[-1]
    n_fox_heads = fox_b_f.shape[1]
    f_pad = -(-n_fox_heads // _LANES) * _LANES - n_fox_heads
    bf = lambda w: w.astype(_BF16)
    Wb = {
        'rwkv_w_r': bf(rwkv_w_r), 'rwkv_w_k': bf(rwkv_w_k), 'rwkv_w_v': bf(rwkv_w_v), 'rwkv_w_o': bf(rwkv_w_o),
        'fox_in': [(bf(w[:, :d]), bf(w[:, d:2 * d]), bf(w[:, 2 * d:3 * d]),
                    bf(jnp.pad(w[:, 3 * d:], ((0, 0), (0, f_pad))))) for w in fox_w_in],
        'fox_w_o': bf(fox_w_o), 'w_up': bf(w_up), 'w_down': bf(w_down),
        'w_ple': bf(w_ple), 'w_ple_gate': bf(w_ple_gate),
    }
    n_rwkv, _, n_heads, n_dim, _ = state_wkv.shape
    b = x_prompt.shape[0]
    shift0 = jnp.zeros((n_rwkv, b, d), x_prompt.dtype)
    wkv0 = jnp.zeros((n_rwkv, b, n_heads, n_dim, n_dim), _F32)
    y_p, shift_p, wkv_p, k_p, v_p, lf_p = _trunk(x_prompt, p_prompt, shift0, wkv0, None, W, Wb)
    sfx, tot = _page_suffix(cache_logf)
    y_s, shift_s, wkv_s, k_s, v_s, lf_s = _trunk(
        x_sample, p_sample, state_shift, state_wkv, (cache_k, cache_v, sfx, tot, page_table), W, Wb)
    return (y_p, y_s, k_p, v_p, lf_p, k_s, v_s, lf_s, wkv_p, shift_p, wkv_s, shift_s)
```

```python
import functools

import jax
import jax.numpy as jnp
from jax import lax
from jax.experimental import pallas as pl
from jax.experimental.pallas import tpu as pltpu

_RMS_EPS = 1e-6
_GN_EPS = 64e-5
_KK_EPS = 1e-24
_NEG = -1e30
_LOG2E = 1.4426950408889634
_LANES = 128
_SUBLANES = 8
_SCAN_STEPS = 32
_ROW_PAD = 8
_VMEM_LIMIT = 52 * 1024 * 1024
_VMEM_BUDGET = 36 * 1024 * 1024
_F32 = jnp.float32
_BF16 = jnp.bfloat16
_NT = (((1,), (1,)), ((), ()))


def _params(*sem):
    return pltpu.CompilerParams(dimension_semantics=sem, vmem_limit_bytes=_VMEM_LIMIT)


def _divisor_tile(n, target, align):
    t = min(n, target)
    t -= t % align
    while t >= align:
        if n % t == 0:
            return t
        t -= align
    return n


def _sigmoid(x):
    return 1.0 / (1.0 + jnp.exp(-x))


def _softplus(x):
    return jnp.maximum(x, 0.0) + jnp.log(1.0 + jnp.exp(-jnp.abs(x)))


def _rms(x, g):
    return x * lax.rsqrt(jnp.mean(x * x, axis=-1, keepdims=True) + _RMS_EPS) * g


def _rmsnorm_kernel(x_ref, g_ref, o_ref):
    o_ref[...] = _rms(x_ref[...], g_ref[...]).astype(o_ref.dtype)


def _rmsnorm(x, g, out_dtype):
    t, d = x.shape
    tr = _divisor_tile(t, 512, 16)
    return pl.pallas_call(
        _rmsnorm_kernel,
        grid=(t // tr,),
        in_specs=[pl.BlockSpec((tr, d), lambda i: (i, 0)), pl.BlockSpec((1, d), lambda i: (0, 0))],
        out_specs=pl.BlockSpec((tr, d), lambda i: (i, 0)),
        out_shape=jax.ShapeDtypeStruct((t, d), out_dtype),
        compiler_params=_params("arbitrary"),
        name="rmsnorm",
    )(x, g.reshape(1, d))


def _mix_kernel(x_ref, xp_ref, sh_ref, g_ref, mu_ref, *o_refs):
    g = g_ref[...]
    xn = _rms(x_ref[0], g)
    prev = _rms(xp_ref[0, 7:8, :], g)
    prev = jnp.where(pl.program_id(1) == 0, sh_ref[0], prev)
    row = lax.broadcasted_iota(jnp.int32, xn.shape, 0)
    xprev = jnp.where(row == 0, prev, pltpu.roll(xn, 1, axis=0))
    xx = xprev - xn
    for c, o_ref in enumerate(o_refs):
        o_ref[0] = (xn + xx * mu_ref[c:c + 1, :]).astype(o_ref.dtype)


def _rwkv_mix(x, shift_prev, g, mu):
    b, l, d = x.shape
    tr = _divisor_tile(l, 256, 8)
    nb8 = tr // 8
    outs = pl.pallas_call(
        _mix_kernel,
        grid=(b, l // tr),
        in_specs=[
            pl.BlockSpec((1, tr, d), lambda bi, i: (bi, i, 0)),
            pl.BlockSpec((1, 8, d), lambda bi, i: (bi, jnp.maximum(i * nb8 - 1, 0), 0)),
            pl.BlockSpec((1, 1, d), lambda bi, i: (bi, 0, 0)),
            pl.BlockSpec((1, d), lambda bi, i: (0, 0)),
            pl.BlockSpec((6, d), lambda bi, i: (0, 0)),
        ],
        out_specs=[pl.BlockSpec((1, tr, d), lambda bi, i: (bi, i, 0))] * 6,
        out_shape=[jax.ShapeDtypeStruct((b, l, d), _BF16)] * 6,
        compiler_params=_params("arbitrary", "arbitrary"),
        name="rwkv_mix",
    )(x, x, shift_prev.reshape(b, 1, d), g.reshape(1, d), mu)
    return [o.reshape(b * l, d) for o in outs]


def _mm_kernel(*refs, n_w, n_emn, n_en, n_skip, epilogue):
    xs = [refs[i][...] for i in range(n_w)]
    ws = [refs[n_w + i][...] for i in range(n_w)]
    pos = 2 * n_w
    emn = [refs[pos + i][...] for i in range(n_emn)]
    pos += n_emn
    en = [refs[pos + i][...] for i in range(n_en)]
    pos += n_en
    accs = [jnp.dot(x, w, preferred_element_type=_F32) for x, w in zip(xs, ws)]
    outs = epilogue(accs, emn, en)
    for o_ref, o in zip(refs[pos + n_skip:], outs):
        o_ref[...] = o.astype(o_ref.dtype)


def _mm_tiles(m, n, x_row_bytes, k_total, n_emn, out_bytes, m_unit):
    best = None
    for tm in (1024, 512, 256, 128, 64, 32, 16, 8):
        if tm > m or m_unit % tm:
            continue
        for tn in (1024, 512, 256, 128):
            if tn > n or n % tn:
                continue
            need = 2 * (tm * x_row_bytes + k_total * tn * 2 + tm * tn * (4 * n_emn + out_bytes)) + tm * tn * 4
            if need > _VMEM_BUDGET:
                continue
            score = (tm * tn / (tm + tn), tm)
            if best is None or score > best[0]:
                best = (score, tm, tn)
    assert best is not None, (m, n)
    return best[1], best[2]


def _mm(xs, ws, *, epilogue, out_dtypes, extras_mn=(), extras_n=(), name="mm", stack=None):
    m = xs[0].shape[0]
    n = ws[0].shape[1]
    x_row_bytes = sum(a.shape[1] * a.dtype.itemsize for a in xs)
    k_total = sum(w.shape[0] for w in ws)
    out_bytes = sum(jnp.dtype(dt).itemsize for dt in out_dtypes)
    tm, tn = _mm_tiles(m, n, x_row_bytes, k_total, len(extras_mn), out_bytes, m if stack is None else stack[2])
    in_specs = [pl.BlockSpec((tm, a.shape[1]), lambda i, j: (i, 0)) for a in xs]
    in_specs += [pl.BlockSpec((w.shape[0], tn), lambda i, j: (0, j)) for w in ws]
    in_specs += [pl.BlockSpec((tm, tn), lambda i, j: (i, j)) for _ in extras_mn]
    in_specs += [pl.BlockSpec((1, tn), lambda i, j: (0, j)) for _ in extras_n]
    kern = functools.partial(_mm_kernel, n_w=len(ws), n_emn=len(extras_mn), n_en=len(extras_n),
                             n_skip=0 if stack is None else 1, epilogue=epilogue)
    out_specs = [pl.BlockSpec((tm, tn), lambda i, j: (i, j)) for _ in out_dtypes]
    out_shape = [jax.ShapeDtypeStruct((m, n), dt) for dt in out_dtypes]
    operands = [*xs, *ws, *extras_mn, *[e.reshape(1, n) for e in extras_n]]
    aliases = {}
    if stack is not None:
        buf, slot, seq = stack
        per = seq // tm
        n_slots = buf.shape[0] // m
        out_specs[0] = pl.BlockSpec((tm, tn), lambda i, j: ((i // per * n_slots + slot) * per + i % per, j))
        out_shape[0] = jax.ShapeDtypeStruct(buf.shape, buf.dtype)
        in_specs.append(pl.BlockSpec(memory_space=pl.ANY))
        aliases = {len(operands): 0}
        operands.append(buf)
    return pl.pallas_call(
        kern,
        grid=(m // tm, n // tn),
        in_specs=in_specs,
        out_specs=out_specs,
        out_shape=out_shape,
        input_output_aliases=aliases,
        compiler_params=_params("arbitrary", "arbitrary"),
        name=name,
    )(*operands)


def _lora_kernel(x_ref, w1_ref, w2_ref, *refs, act, n_en, epilogue):
    h = jnp.dot(x_ref[...], w1_ref[...], preferred_element_type=_F32)
    acc = jnp.dot(act(h).astype(_BF16), w2_ref[...], preferred_element_type=_F32)
    en = [refs[i][...] for i in range(n_en)]
    refs[n_en][...] = epilogue(acc, en).astype(refs[n_en].dtype)


def _lora(x, w1, w2, *, act, epilogue, extras_n=(), name="lora"):
    m, k = x.shape
    r = w1.shape[1]
    n = w2.shape[1]
    rp = -(-r // _LANES) * _LANES
    w1 = jnp.pad(w1, ((0, 0), (0, rp - r))).astype(_BF16)
    w2 = jnp.pad(w2, ((0, rp - r), (0, 0))).astype(_BF16)
    tm = _divisor_tile(m, 512, 16)
    kern = functools.partial(_lora_kernel, act=act, n_en=len(extras_n), epilogue=epilogue)
    return pl.pallas_call(
        kern,
        grid=(m // tm,),
        in_specs=[pl.BlockSpec((tm, k), lambda i: (i, 0)),
                  pl.BlockSpec((k, rp), lambda i: (0, 0)),
                  pl.BlockSpec((rp, n), lambda i: (0, 0))]
                 + [pl.BlockSpec((1, n), lambda i: (0, 0)) for _ in extras_n],
        out_specs=pl.BlockSpec((tm, n), lambda i: (i, 0)),
        out_shape=jax.ShapeDtypeStruct((m, n), _F32),
        compiler_params=_params("arbitrary"),
        name=name,
    )(x, w1, w2, *[e.reshape(1, n) for e in extras_n])


def _scan_kernel(r_ref, w_ref, k_ref, v_ref, a_ref, kkp_ref, kap_ref, rk_ref, gw_ref, gb_ref, s0_ref,
                 o_ref, s_ref, kk_s, kka_s, km_s, rr_s, gam_s, *, tb, n):
    @pl.when(pl.program_id(1) == 0)
    def _():
        s_ref[...] = s0_ref[...]

    gam_s[...] = jnp.ones(gam_s.shape, _F32)

    def step(t, carry):
        k_t = k_ref[t, 0:n, :]
        a_t = a_ref[t, 0:n, :]
        r_t = r_ref[t, 0:n, :]
        kk = k_t * kkp_ref[...]
        kk = kk * lax.rsqrt(jnp.maximum(jnp.sum(kk * kk, axis=0, keepdims=True), _KK_EPS))
        km = k_t * (1.0 + (a_t - 1.0) * kap_ref[...])
        gam_prev = gam_s[...]
        gam = gam_prev * w_ref[t, 0:n, :]
        inv = 1.0 / gam
        gam_s[...] = gam
        kk_s[...] = kk * gam_prev
        kka_s[...] = kk * a_t * inv
        km_s[...] = km * inv
        rr_s[...] = r_t * gam
        v_t = v_ref[t, 0:n, :]
        sk = [jnp.zeros_like(v_t), jnp.zeros_like(v_t)]
        for j in range(n):
            sk[j % 2] = sk[j % 2] + s_ref[j] * kk_s[j:j + 1, :]
        sk = sk[0] + sk[1]
        o = [jnp.zeros_like(v_t), jnp.zeros_like(v_t)]
        for j in range(n):
            s_new = s_ref[j] + (v_t * km_s[j:j + 1, :] - sk * kka_s[j:j + 1, :])
            s_ref[j] = s_new
            o[j % 2] = o[j % 2] + s_new * rr_s[j:j + 1, :]
        o = o[0] + o[1]
        mean = jnp.mean(o, axis=0, keepdims=True)
        var = jnp.mean(jnp.square(o - mean), axis=0, keepdims=True)
        o = (o - mean) * lax.rsqrt(var + _GN_EPS) * gw_ref[...] + gb_ref[...]
        bonus = jnp.sum(r_t * km * rk_ref[...], axis=0, keepdims=True) * v_t
        o_ref[t, 0:n, :] = o + bonus
        o_ref[t, n:, :] = jnp.zeros((o_ref.shape[1] - n, _LANES), _F32)
        return carry

    lax.fori_loop(0, tb, step, 0)
    for j in range(n):
        s_ref[j] = s_ref[j] * gam_s[j:j + 1, :]


def _rwkv_scan(seqs, params, s0, n):
    l, rows, p = seqs[0].shape
    tb = _divisor_tile(l, _SCAN_STEPS, 1)
    seq_spec = pl.BlockSpec((tb, rows, _LANES), lambda c, i: (i, 0, c))
    par_spec = pl.BlockSpec((n, _LANES), lambda c, i: (0, c))
    st_spec = pl.BlockSpec((n, n, _LANES), lambda c, i: (0, 0, c))
    return pl.pallas_call(
        functools.partial(_scan_kernel, tb=tb, n=n),
        grid=(p // _LANES, l // tb),
        in_specs=[seq_spec] * 5 + [par_spec] * 5 + [st_spec],
        out_specs=[seq_spec, st_spec],
        out_shape=[jax.ShapeDtypeStruct((l, rows, p), _F32), jax.ShapeDtypeStruct((n, n, p), _F32)],
        scratch_shapes=[pltpu.VMEM((n, _LANES), _F32)] * 5,
        compiler_params=_params("arbitrary", "arbitrary"),
        name="rwkv_scan",
    )(*seqs, *params, s0)


def _lanes_in_pallas(b, l, h, n):
    return l % _LANES == 0 and _LANES % h == 0 and b % (_LANES // h) == 0 and _LANES % n == 0


def _to_lanes_kernel(x_ref, o_ref, y_s, *, n):
    g, steps, d = x_ref.shape
    tiles = d // _LANES
    per = _LANES // n
    pitch = o_ref.shape[0] // steps
    for b in range(g):
        for c in range(tiles):
            t = x_ref[b, :, c * _LANES:(c + 1) * _LANES].T
            for hh in range(per):
                r0 = ((b * tiles + c) * per + hh) * pitch
                y_s[r0:r0 + n, :] = t[hh * n:(hh + 1) * n, :]
    for j in range(n):
        o_ref[pl.ds(j, steps, stride=pitch), :] = y_s[pl.ds(j, _LANES, stride=pitch), :].T
    for j in range(n, pitch):
        o_ref[pl.ds(j, steps, stride=pitch), :] = jnp.zeros((steps, _LANES), _F32)


def _from_lanes_kernel(o_ref, g_ref, out_ref, y_s, *, n):
    g, steps, d = g_ref.shape
    tiles = d // _LANES
    per = _LANES // n
    pitch = o_ref.shape[0] // steps
    for j in range(n):
        y_s[pl.ds(j, _LANES, stride=pitch), :] = o_ref[pl.ds(j, steps, stride=pitch), :].T
    for b in range(g):
        for c in range(tiles):
            t = jnp.concatenate([y_s[((b * tiles + c) * per + hh) * pitch:((b * tiles + c) * per + hh) * pitch + n, :]
                                 for hh in range(per)], axis=0)
            cols = slice(c * _LANES, (c + 1) * _LANES)
            out_ref[b, :, cols] = (t.T * g_ref[b, :, cols]).astype(out_ref.dtype)


def _to_lanes(x, b, l, h, n, p):
    d = h * n
    rows = n + _ROW_PAD
    if not _lanes_in_pallas(b, l, h, n):
        y = x.reshape(b, l, h, n).transpose(1, 3, 0, 2).reshape(l, n, b * h)
        return jnp.pad(y, ((0, 0), (0, _ROW_PAD), (0, p - b * h)))
    g = _LANES // h
    out = pl.pallas_call(
        functools.partial(_to_lanes_kernel, n=n),
        grid=(p // _LANES, l // _LANES),
        in_specs=[pl.BlockSpec((g, _LANES, d), lambda c, i: (c, i, 0))],
        out_specs=pl.BlockSpec((_LANES * rows, _LANES), lambda c, i: (i, c)),
        out_shape=jax.ShapeDtypeStruct((l * rows, p), _F32),
        scratch_shapes=[pltpu.VMEM((_LANES * rows, _LANES), _F32)],
        compiler_params=_params("arbitrary", "arbitrary"),
        name="to_lanes",
    )(x.reshape(b, l, d))
    return out.reshape(l, rows, p)


def _from_lanes(o, gate, b, l, h, n):
    d = h * n
    rows, p = o.shape[1:]
    if not _lanes_in_pallas(b, l, h, n):
        y = o[:, :n, :b * h].reshape(l, n, b, h).transpose(2, 0, 3, 1).reshape(b * l, d)
        return (y * gate).astype(_BF16)
    g = _LANES // h
    out = pl.pallas_call(
        functools.partial(_from_lanes_kernel, n=n),
        grid=(p // _LANES, l // _LANES),
        in_specs=[pl.BlockSpec((_LANES * rows, _LANES), lambda c, i: (i, c)),
                  pl.BlockSpec((g, _LANES, d), lambda c, i: (c, i, 0))],
        out_specs=pl.BlockSpec((g, _LANES, d), lambda c, i: (c, i, 0)),
        out_shape=jax.ShapeDtypeStruct((b, l, d), _BF16),
        scratch_shapes=[pltpu.VMEM((_LANES * rows, _LANES), _F32)],
        compiler_params=_params("arbitrary", "arbitrary"),
        name="from_lanes",
    )(o.reshape(l * rows, p), gate.reshape(b, l, d))
    return out.reshape(b * l, d)


def _param_lanes(w, b, h, n, p):
    y = jnp.broadcast_to(w.reshape(h, n).T[:, None, :], (n, b, h)).reshape(n, b * h)
    return jnp.pad(y, ((0, 0), (0, p - b * h)))


def _cumsum_kernel(x_ref, o_ref, *, scale):
    x = x_ref[...] * scale
    length = x.shape[1]
    col = lax.broadcasted_iota(jnp.int32, x.shape, 1)
    if length % _LANES == 0:
        s = 1
        while s < length:
            x = x + jnp.where(col >= s, pltpu.roll(x, s, axis=1), 0.0)
            s *= 2
        o_ref[...] = x
    else:
        acc = jnp.zeros_like(x)
        for s in range(length):
            acc = acc + jnp.where(col >= s, x[:, s:s + 1], 0.0)
        o_ref[...] = acc


def _cumsum_rows(x, scale):
    return pl.pallas_call(
        functools.partial(_cumsum_kernel, scale=scale),
        out_shape=jax.ShapeDtypeStruct(x.shape, _F32),
        compiler_params=pltpu.CompilerParams(vmem_limit_bytes=_VMEM_LIMIT),
        name="logf_cumsum",
    )(x)


def _fox_kernel(q_ref, k_ref, v_ref, c_ref, o_ref, *, tq, dh, n_grp):
    i = pl.program_id(2)
    row = lax.broadcasted_iota(jnp.int32, (tq, tq), 0)
    col = lax.broadcasted_iota(jnp.int32, (tq, tq), 1)
    q0 = pl.multiple_of(i * tq, tq)
    qs = [q_ref[0, :, g * dh:(g + 1) * dh] for g in range(n_grp)]
    cqs = [jnp.sum(jnp.where(row == col, c_ref[g, :, pl.ds(q0, tq)], 0.0), axis=1, keepdims=True)
           for g in range(n_grp)]

    def block(j, carry, masked):
        k0 = pl.multiple_of(j * tq, tq)
        out = []
        for g in range(n_grp):
            m, l, acc = carry[g]
            lanes = slice(g * dh, (g + 1) * dh)
            s = lax.dot_general(qs[g], k_ref[0, pl.ds(k0, tq), lanes], _NT, preferred_element_type=_F32)
            u = s - c_ref[g, :, pl.ds(k0, tq)]
            if masked:
                u = jnp.where(col <= row, u, _NEG)
            m_new = jnp.maximum(m, jnp.max(u, axis=1, keepdims=True) + cqs[g])
            alpha = jnp.exp2(m - m_new)
            p = jnp.exp2(u - (m_new - cqs[g]))
            l = alpha * l + jnp.sum(p, axis=1, keepdims=True)
            acc = alpha * acc + jnp.dot(p.astype(_BF16), v_ref[0, pl.ds(k0, tq), lanes],
                                        preferred_element_type=_F32)
            out.append((m_new, l, acc))
        return tuple(out)

    init = tuple((jnp.full((tq, 1), _NEG, _F32), jnp.zeros((tq, 1), _F32), jnp.zeros((tq, dh), _F32))
                 for _ in range(n_grp))
    carry = lax.fori_loop(0, i, lambda j, c: block(j, c, False), init)
    carry = block(i, carry, True)
    for g in range(n_grp):
        _, l, acc = carry[g]
        o_ref[0, :, g * dh:(g + 1) * dh] = (acc / l).astype(o_ref.dtype)


def _fox_attention(q, k, v, c_rows, b, l, h, dh):
    tq = _divisor_tile(l, 512, _LANES)
    n_grp = 2 if h % 2 == 0 else 1
    qo_spec = pl.BlockSpec((1, tq, n_grp * dh), lambda bi, hi, i: (bi, i, hi))
    kv_spec = pl.BlockSpec((1, l, n_grp * dh), lambda bi, hi, i: (bi, 0, hi))
    c_spec = pl.BlockSpec((n_grp, 1, l), lambda bi, hi, i: (bi * (h // n_grp) + hi, 0, 0))
    return pl.pallas_call(
        functools.partial(_fox_kernel, tq=tq, dh=dh, n_grp=n_grp),
        grid=(b, h // n_grp, l // tq),
        in_specs=[qo_spec, kv_spec, kv_spec, c_spec],
        out_specs=qo_spec,
        out_shape=jax.ShapeDtypeStruct((b, l, h * dh), _BF16),
        compiler_params=_params("arbitrary", "arbitrary", "arbitrary"),
        name="fox_attention",
    )(q, k, v, c_rows)


def _suffix_kernel(x_ref, sfx_ref, tot_ref, *, stride):
    x = x_ref[...]
    n = x.shape[1]
    col = lax.broadcasted_iota(jnp.int32, x.shape, 1)
    inc = x
    tot = x
    s = stride
    while s < n:
        inc = inc + jnp.where(col + s < n, pltpu.roll(inc, n - s, axis=1), 0.0)
        tot = tot + pltpu.roll(tot, n - s, axis=1)
        s *= 2
    sfx_ref[...] = inc - x
    tot_ref[...] = tot


def _page_suffix(cache_logf):
    n_phys, n_fox, ps, h = cache_logf.shape
    assert ps & (ps - 1) == 0 and (ps * h) % _LANES == 0, (ps, h)
    rows = n_phys * n_fox
    tr = _divisor_tile(rows, 256, 8)
    spec = pl.BlockSpec((tr, ps * h), lambda i: (i, 0))
    sfx, tot = pl.pallas_call(
        functools.partial(_suffix_kernel, stride=h),
        grid=(rows // tr,),
        in_specs=[spec],
        out_specs=[spec, spec],
        out_shape=[jax.ShapeDtypeStruct((rows, ps * h), _F32)] * 2,
        compiler_params=_params("arbitrary"),
        name="page_suffix",
    )(cache_logf.reshape(rows, ps * h))
    return sfx, tot


def _decode_kernel(pt_ref, q_ref, cq_ref, cn_ref, kn_ref, vn_ref, mask_ref, *refs, n_heads, n_q, dh, n_grp, n_steps,
                   n_pages, n_layers, layer):
    kps, vps = refs[:n_grp], refs[n_grp:2 * n_grp]
    sfs, tts = refs[2 * n_grp:3 * n_grp], refs[3 * n_grp:4 * n_grp]
    o_ref, m_s, l_s, acc_s, carry_s = refs[4 * n_grp:]
    p = pl.program_id(1)
    q = q_ref[0]
    cq = cq_ref[0]

    def update(us, vs):
        m_old = m_s[...]
        mu = functools.reduce(jnp.maximum, [jnp.max(u, axis=1, keepdims=True) for u in us])
        m_new = jnp.maximum(m_old, mu + cq)
        shift = m_new - cq
        alpha = jnp.exp(m_old - m_new)
        l = alpha * l_s[...]
        acc = alpha * acc_s[...]
        for u, v in zip(us, vs):
            pr = jnp.exp(u - shift)
            l = l + jnp.sum(pr, axis=1, keepdims=True)
            acc = acc + jnp.dot(pr.astype(_BF16), v, preferred_element_type=_F32)
        m_s[...] = m_new
        l_s[...] = l
        acc_s[...] = acc

    @pl.when(p == 0)
    def _():
        m_s[...] = jnp.full(m_s.shape, _NEG, _F32)
        l_s[...] = jnp.zeros(l_s.shape, _F32)
        acc_s[...] = jnp.zeros(acc_s.shape, _F32)
        carry_s[...] = jnp.zeros(carry_s.shape, _F32)
        s = lax.dot_general(q, kn_ref[0], _NT, preferred_element_type=_F32)
        row = lax.broadcasted_iota(jnp.int32, s.shape, 0)
        col = lax.broadcasted_iota(jnp.int32, s.shape, 1)
        valid = (lax.rem(col, n_heads) == row // n_q) & (col // n_heads <= lax.rem(row, n_q))
        update([jnp.where(valid, s - cn_ref[0], _NEG)], [vn_ref[0]])

    @pl.when(p > 0)
    def _():
        carry = carry_s[...]
        us, vs = [], []
        for g in range(n_grp):
            page = pt_ref[pl.program_id(0), n_pages - 1 - (p - 1) * n_grp - g]
            row = pl.ds(lax.rem(page * n_layers + layer, _SUBLANES), 1)
            s = lax.dot_general(q, kps[g][...].astype(_BF16), _NT, preferred_element_type=_F32)
            us.append(s + ((sfs[g][row, :] + carry) + mask_ref[...]))
            carry = carry + tts[g][row, :]
            vs.append(vps[g][...].astype(_BF16))
        carry_s[...] = carry
        update(us, vs)

    @pl.when(p == n_steps - 1)
    def _():
        acc = acc_s[...] / l_s[...]
        for hh in range(n_heads):
            o_ref[0, :, hh * dh:(hh + 1) * dh] = acc[hh * n_q:(hh + 1) * n_q, :]


def _fox_decode(q, k_new, v_new, c_new, cache_k, cache_v, sfx, tot, page_table, layer):
    b, nq, h, dh = q.shape
    n_phys, n_fox, ps = cache_k.shape[:3]
    n_pages = page_table.shape[1]
    r = h * nq
    n_grp = next(g for g in (4, 2, 1) if n_pages % g == 0)
    n_steps = n_pages // n_grp + 1
    qs = (q * dh ** -0.5).transpose(0, 2, 1, 3).reshape(b, r, dh).astype(_BF16)
    cq = c_new.transpose(0, 2, 1).reshape(b, r, 1)
    row = lax.broadcasted_iota(jnp.int32, (r, ps * h), 0)
    col = lax.broadcasted_iota(jnp.int32, (r, ps * h), 1)
    mask = jnp.where(col % h == row // nq, 0.0, _NEG).astype(_F32)

    page_of = lambda bi, p, pt, g: pt[bi, n_pages - 1 - (jnp.maximum(p, 1) - 1) * n_grp - g]

    def page_spec(g):
        return pl.BlockSpec((None, None, ps * h, dh), lambda bi, p, pt: (page_of(bi, p, pt, g), layer, 0, 0))

    def group_spec(g):
        return pl.BlockSpec((_SUBLANES, ps * h),
                            lambda bi, p, pt: ((page_of(bi, p, pt, g) * n_fox + layer) // _SUBLANES, 0))

    per_seq = lambda shape: pl.BlockSpec((1,) + shape, lambda bi, p, pt: (bi, 0, 0))
    grid_spec = pltpu.PrefetchScalarGridSpec(
        num_scalar_prefetch=1,
        grid=(b, n_steps),
        in_specs=[per_seq((r, dh)), per_seq((r, 1)), per_seq((1, nq * h)), per_seq((nq * h, dh)),
                  per_seq((nq * h, dh)), pl.BlockSpec((r, ps * h), lambda bi, p, pt: (0, 0))]
                 + [page_spec(g) for g in range(n_grp)] * 2
                 + [group_spec(g) for g in range(n_grp)] * 2,
        out_specs=per_seq((nq, h * dh)),
        scratch_shapes=[pltpu.VMEM((r, 1), _F32), pltpu.VMEM((r, 1), _F32), pltpu.VMEM((r, dh), _F32),
                        pltpu.VMEM((1, ps * h), _F32)],
    )
    kc = cache_k.reshape(n_phys, n_fox, ps * h, dh)
    vc = cache_v.reshape(n_phys, n_fox, ps * h, dh)
    out = pl.pallas_call(
        functools.partial(_decode_kernel, n_heads=h, n_q=nq, dh=dh, n_grp=n_grp, n_steps=n_steps,
                          n_pages=n_pages, n_layers=n_fox, layer=layer),
        grid_spec=grid_spec,
        out_shape=jax.ShapeDtypeStruct((b, nq, h * dh), _F32),
        compiler_params=_params("arbitrary", "arbitrary"),
        name="fox_decode",
    )(page_table, qs, cq, c_new.reshape(b, 1, nq * h), k_new.reshape(b, nq * h, dh).astype(_BF16),
      v_new.reshape(b, nq * h, dh).astype(_BF16), mask, *([kc] * n_grp), *([vc] * n_grp),
      *([sfx] * n_grp), *([tot] * n_grp))
    return out.reshape(b * nq, h * dh).astype(_BF16)


def _first(accs, emn, en):
    return [accs[0]]


def _rwkv_layer(x, shift_prev, wkv_prev, v_first, j, i, W, Wb):
    b, l, d = x.shape
    h, n = wkv_prev.shape[1], wkv_prev.shape[2]
    xr, xw, xk, xv, xa, xg = _rwkv_mix(x, shift_prev, W['norm_mix'][i], W['rwkv_mu'][j])
    (r,) = _mm([xr], [Wb['rwkv_w_r'][j]], epilogue=_first, out_dtypes=[_F32], name="rwkv_r")
    decay = _lora(xw, W['rwkv_w1'][j], W['rwkv_w2'][j], act=jnp.tanh, extras_n=[W['rwkv_w0'][j]],
                  epilogue=lambda acc, en: jnp.exp(-jnp.exp(-_softplus(-(en[0] + acc)) - 0.5)), name="rwkv_decay")
    a = _lora(xa, W['rwkv_a1'][j], W['rwkv_a2'][j], act=lambda t: t, extras_n=[W['rwkv_a0'][j]],
              epilogue=lambda acc, en: _sigmoid(en[0] + acc), name="rwkv_a")
    g = _lora(xg, W['rwkv_g1'][j], W['rwkv_g2'][j], act=_sigmoid, epilogue=lambda acc, en: acc, name="rwkv_g")
    if j == 0:
        (v,) = _mm([xv], [Wb['rwkv_w_v'][j]], epilogue=_first, out_dtypes=[_F32], name="rwkv_v")
        v_first = v
    else:
        gate = _lora(xv, W['rwkv_v1'][j - 1], W['rwkv_v2'][j - 1], act=lambda t: t, extras_n=[W['rwkv_v0'][j - 1]],
                     epilogue=lambda acc, en: _sigmoid(en[0] + acc), name="rwkv_vgate")
        (v,) = _mm([xv], [Wb['rwkv_w_v'][j]], extras_mn=[v_first, gate], out_dtypes=[_F32], name="rwkv_v",
                   epilogue=lambda accs, emn, en: [accs[0] + (emn[0] - accs[0]) * emn[1]])
    (k,) = _mm([xk], [Wb['rwkv_w_k'][j]], epilogue=_first, out_dtypes=[_F32], name="rwkv_k")
    p = -(-(b * h) // _LANES) * _LANES
    seqs = [_to_lanes(t, b, l, h, n, p) for t in (r, decay, k, v, a)]
    params = [_param_lanes(w.reshape(-1), b, h, n, p) for w in
              (W['rwkv_k_k'][j], W['rwkv_k_a'][j], W['rwkv_r_k'][j], W['rwkv_gn_w'][j], W['rwkv_gn_b'][j])]
    s0 = jnp.pad(wkv_prev.astype(_F32).transpose(3, 2, 0, 1).reshape(n, n, b * h), ((0, 0), (0, 0), (0, p - b * h)))
    o, s_last = _rwkv_scan(seqs, params, s0, n)
    og = _from_lanes(o, g, b, l, h, n)
    s_last = s_last[:, :, :b * h].reshape(n, n, b, h).transpose(2, 3, 1, 0)
    x2 = x.reshape(b * l, d)
    (x2,) = _mm([og], [Wb['rwkv_w_o'][j]], extras_mn=[x2], out_dtypes=[_F32], name="rwkv_out",
                epilogue=lambda accs, emn, en: [emn[0] + accs[0]])
    shift_out = _rmsnorm(x[:, -1, :], W['norm_mix'][i], _F32)
    return x2.reshape(b, l, d), shift_out, s_last, v_first


def _fox_layer(x, j, i, W, Wb, paged, kv_bufs):
    b, l, d = x.shape
    h = W['fox_b_f'].shape[1]
    dh = d // h
    x2 = x.reshape(b * l, d)
    hn = _rmsnorm(x2, W['norm_mix'][i], _BF16)
    w_q, w_k, w_v, w_f = Wb['fox_in'][j]
    if paged is None:
        q_scale = dh ** -0.5 * _LOG2E
        (q,) = _mm([hn], [w_q], epilogue=lambda accs, emn, en: [accs[0] * q_scale], out_dtypes=[_BF16], name="fox_q")
    else:
        (q,) = _mm([hn], [w_q], epilogue=_first, out_dtypes=[_F32], name="fox_q")
    twice = lambda accs, emn, en: [accs[0], accs[0]]
    stack_k, stack_v = (None, None) if kv_bufs is None else ((kv_bufs[0], j, l), (kv_bufs[1], j, l))
    k, kb = _mm([hn], [w_k], epilogue=twice, out_dtypes=[_F32, _BF16], name="fox_k", stack=stack_k)
    v, vb = _mm([hn], [w_v], epilogue=twice, out_dtypes=[_F32, _BF16], name="fox_v", stack=stack_v)
    bias_f = jnp.pad(W['fox_b_f'][j], (0, w_f.shape[1] - h))
    (logf_pad,) = _mm([hn], [w_f], extras_n=[bias_f], out_dtypes=[_F32], name="fox_logf",
                      epilogue=lambda accs, emn, en: [-_softplus(-(accs[0] + en[0]))])
    logf = logf_pad[:, :h].reshape(b, l, h)
    c_rows = _cumsum_rows(logf.transpose(0, 2, 1).reshape(b * h, l), _LOG2E if paged is None else 1.0)
    if paged is None:
        o = _fox_attention(q.reshape(b, l, d), kb.reshape(b, l, d), vb.reshape(b, l, d),
                           c_rows.reshape(b * h, 1, l), b, l, h, dh)
        o = o.reshape(b * l, d)
    else:
        cache_k, cache_v, sfx, tot, page_table = paged
        c_new = c_rows.reshape(b, h, l).transpose(0, 2, 1)
        o = _fox_decode(q.reshape(b, l, h, dh), k.reshape(b, l, h, dh), v.reshape(b, l, h, dh), c_new,
                        cache_k, cache_v, sfx, tot, page_table, j)
    (x2,) = _mm([o], [Wb['fox_w_o'][j]], extras_mn=[x2], out_dtypes=[_F32], name="fox_out",
                epilogue=lambda accs, emn, en: [emn[0] + accs[0]])
    return x2.reshape(b, l, d), k, v, logf


def _ffn_ple(x, p_i, i, W, Wb):
    b, l, d = x.shape
    x2 = x.reshape(b * l, d)
    hn = _rmsnorm(x2, W['norm_ffn'][i], _BF16)
    (up,) = _mm([hn], [Wb['w_up'][i]], out_dtypes=[_BF16], name="ffn_up",
                epilogue=lambda accs, emn, en: [jnp.square(jnp.maximum(accs[0], 0.0))])
    (x2,) = _mm([up], [Wb['w_down'][i]], extras_mn=[x2], out_dtypes=[_F32], name="ffn_down",
                epilogue=lambda accs, emn, en: [emn[0] + accs[0]])
    hp = _rmsnorm(x2, W['norm_ple'][i], _BF16)
    pb = p_i.reshape(b * l, -1).astype(_BF16)
    (x2,) = _mm([hp, pb], [Wb['w_ple_gate'][i], Wb['w_ple'][i]], extras_mn=[x2], out_dtypes=[_F32], name="ple",
                epilogue=lambda accs, emn, en: [emn[0] + _sigmoid(accs[0]) * accs[1]])
    return x2.reshape(b, l, d)


def _trunk(x, p, shift_in, wkv_in, paged, W, Wb):
    depth = W['norm_mix'].shape[0]
    b, l, d = x.shape
    n_fox = depth // 2
    h = W['fox_b_f'].shape[1]
    kv_bufs = None
    if paged is None and l % _LANES == 0:
        kv_bufs = [jnp.zeros((b * n_fox * l, d), _F32), jnp.zeros((b * n_fox * l, d), _F32)]
    v_first = None
    shifts, wkvs, ks, vs, lfs = [], [], [], [], []
    for i in range(depth):
        j = i // 2
        if i % 2 == 0:
            x, sh, s_last, v_first = _rwkv_layer(x, shift_in[j], wkv_in[j], v_first, j, i, W, Wb)
            shifts.append(sh)
            wkvs.append(s_last)
        else:
            x, k, v, lf = _fox_layer(x, j, i, W, Wb, paged, kv_bufs)
            if kv_bufs is not None:
                kv_bufs = [k, v]
            else:
                ks.append(k.reshape(b, l, h, d // h))
                vs.append(v.reshape(b, l, h, d // h))
            lfs.append(lf)
        x = _ffn_ple(x, p[i], i, W, Wb)
    y = _rmsnorm(x.reshape(b * l, d), W['norm_out'], _F32).reshape(b, l, d)
    if kv_bufs is None:
        k_all, v_all = jnp.stack(ks, 1), jnp.stack(vs, 1)
    else:
        k_all, v_all = (t.reshape(b, n_fox, l, h, d // h) for t in kv_bufs)
    return y, jnp.stack(shifts, 0), jnp.stack(wkvs, 0), k_all, v_all, jnp.stack(lfs, 1)


def kernel(x_prompt, x_sample, cache_k, cache_v, cache_logf, state_wkv, state_shift, page_table, p_prompt, p_sample, norm_mix, norm_ffn, norm_ple, norm_out, rwkv_mu, rwkv_w_r, rwkv_w_k, rwkv_w_v, rwkv_w_o, rwkv_w0, rwkv_w1, rwkv_w2, rwkv_a0, rwkv_a1, rwkv_a2, rwkv_v0, rwkv_v1, rwkv_v2, rwkv_g1, rwkv_g2, rwkv_k_k, rwkv_k_a, rwkv_r_k, rwkv_gn_w, rwkv_gn_b, fox_w_in, fox_b_f, fox_w_o, w_up, w_down, w_ple, w_ple_gate):
    W = {
        'norm_mix': norm_mix, 'norm_ffn': norm_ffn, 'norm_ple': norm_ple, 'norm_out': norm_out,
        'rwkv_mu': rwkv_mu, 'rwkv_w0': rwkv_w0, 'rwkv_w1': rwkv_w1, 'rwkv_w2': rwkv_w2,
        'rwkv_a0': rwkv_a0, 'rwkv_a1': rwkv_a1, 'rwkv_a2': rwkv_a2,
        'rwkv_v0': rwkv_v0, 'rwkv_v1': rwkv_v1, 'rwkv_v2': rwkv_v2,
        'rwkv_g1': rwkv_g1, 'rwkv_g2': rwkv_g2, 'rwkv_k_k': rwkv_k_k, 'rwkv_k_a': rwkv_k_a,
        'rwkv_r_k': rwkv_r_k, 'rwkv_gn_w': rwkv_gn_w, 'rwkv_gn_b': rwkv_gn_b, 'fox_b_f': fox_b_f,
    }
    d = x_prompt.shape[-1]
    n_fox_heads = fox_b_f.shape[1]
    f_pad = -(-n_fox_heads // _LANES) * _LANES - n_fox_heads
    bf = lambda w: w.astype(_BF16)
    Wb = {
        'rwkv_w_r': bf(rwkv_w_r), 'rwkv_w_k': bf(rwkv_w_k), 'rwkv_w_v': bf(rwkv_w_v), 'rwkv_w_o': bf(rwkv_w_o),
        'fox_in': [(bf(w[:, :d]), bf(w[:, d:2 * d]), bf(w[:, 2 * d:3 * d]),
                    bf(jnp.pad(w[:, 3 * d:], ((0, 0), (0, f_pad))))) for w in fox_w_in],
        'fox_w_o': bf(fox_w_o), 'w_up': bf(w_up), 'w_down': bf(w_down),
        'w_ple': bf(w_ple), 'w_ple_gate': bf(w_ple_gate),
    }
    n_rwkv, _, n_heads, n_dim, _ = state_wkv.shape
    b = x_prompt.shape[0]
    shift0 = jnp.zeros((n_rwkv, b, d), x_prompt.dtype)
    wkv0 = jnp.zeros((n_rwkv, b, n_heads, n_dim, n_dim), _F32)
    y_p, shift_p, wkv_p, k_p, v_p, lf_p = _trunk(x_prompt, p_prompt, shift0, wkv0, None, W, Wb)
    sfx, tot = _page_suffix(cache_logf)
    y_s, shift_s, wkv_s, k_s, v_s, lf_s = _trunk(
        x_sample, p_sample, state_shift, state_wkv, (cache_k, cache_v, sfx, tot, page_table), W, Wb)
    return (y_p, y_s, k_p, v_p, lf_p, k_s, v_s, lf_s, wkv_p, shift_p, wkv_s, shift_s)
```

```python
import functools

import jax
import jax.numpy as jnp
from jax import lax
from jax.experimental import pallas as pl
from jax.experimental.pallas import tpu as pltpu

_RMS_EPS = 1e-6
_GN_EPS = 64e-5
_KK_EPS = 1e-24
_NEG = -1e30
_LOG2E = 1.4426950408889634
_LANES = 128
_SUBLANES = 8
_SCAN_STEPS = 64
_ROW_PAD = 8
_VMEM_LIMIT = 56 * 1024 * 1024
_VMEM_BUDGET = 40 * 1024 * 1024
_F32 = jnp.float32
_BF16 = jnp.bfloat16
_NT = (((1,), (1,)), ((), ()))


def _params(*sem):
    return pltpu.CompilerParams(dimension_semantics=sem, vmem_limit_bytes=_VMEM_LIMIT)


def _divisor_tile(n, target, align):
    t = min(n, target)
    t -= t % align
    while t >= align:
        if n % t == 0:
            return t
        t -= align
    return n


def _sigmoid(x):
    return 1.0 / (1.0 + jnp.exp(-x))


def _softplus(x):
    return jnp.maximum(x, 0.0) + jnp.log(1.0 + jnp.exp(-jnp.abs(x)))


def _rms(x, g):
    return x * lax.rsqrt(jnp.mean(x * x, axis=-1, keepdims=True) + _RMS_EPS) * g


def _rmsnorm_kernel(x_ref, g_ref, o_ref):
    o_ref[...] = _rms(x_ref[...], g_ref[...]).astype(o_ref.dtype)


def _rmsnorm(x, g, out_dtype):
    t, d = x.shape
    tr = _divisor_tile(t, 512, 16)
    return pl.pallas_call(
        _rmsnorm_kernel,
        grid=(t // tr,),
        in_specs=[pl.BlockSpec((tr, d), lambda i: (i, 0)), pl.BlockSpec((1, d), lambda i: (0, 0))],
        out_specs=pl.BlockSpec((tr, d), lambda i: (i, 0)),
        out_shape=jax.ShapeDtypeStruct((t, d), out_dtype),
        compiler_params=_params("arbitrary"),
        name="rmsnorm",
    )(x, g.reshape(1, d))


def _mix_kernel(x_ref, xp_ref, sh_ref, g_ref, mu_ref, *o_refs):
    g = g_ref[...]
    xn = _rms(x_ref[0], g)
    prev = _rms(xp_ref[0, 7:8, :], g)
    prev = jnp.where(pl.program_id(1) == 0, sh_ref[0], prev)
    row = lax.broadcasted_iota(jnp.int32, xn.shape, 0)
    xprev = jnp.where(row == 0, prev, pltpu.roll(xn, 1, axis=0))
    xx = xprev - xn
    for c, o_ref in enumerate(o_refs):
        o_ref[0] = (xn + xx * mu_ref[c:c + 1, :]).astype(o_ref.dtype)


def _rwkv_mix(x, shift_prev, g, mu):
    b, l, d = x.shape
    tr = _divisor_tile(l, 256, 8)
    nb8 = tr // 8
    outs = pl.pallas_call(
        _mix_kernel,
        grid=(b, l // tr),
        in_specs=[
            pl.BlockSpec((1, tr, d), lambda bi, i: (bi, i, 0)),
            pl.BlockSpec((1, 8, d), lambda bi, i: (bi, jnp.maximum(i * nb8 - 1, 0), 0)),
            pl.BlockSpec((1, 1, d), lambda bi, i: (bi, 0, 0)),
            pl.BlockSpec((1, d), lambda bi, i: (0, 0)),
            pl.BlockSpec((6, d), lambda bi, i: (0, 0)),
        ],
        out_specs=[pl.BlockSpec((1, tr, d), lambda bi, i: (bi, i, 0))] * 6,
        out_shape=[jax.ShapeDtypeStruct((b, l, d), _BF16)] * 6,
        compiler_params=_params("arbitrary", "arbitrary"),
        name="rwkv_mix",
    )(x, x, shift_prev.reshape(b, 1, d), g.reshape(1, d), mu)
    return [o.reshape(b * l, d) for o in outs]


def _mm_kernel(*refs, n_w, n_emn, n_en, n_skip, epilogue):
    xs = [refs[i][...] for i in range(n_w)]
    ws = [refs[n_w + i][...] for i in range(n_w)]
    pos = 2 * n_w
    emn = [refs[pos + i][...] for i in range(n_emn)]
    pos += n_emn
    en = [refs[pos + i][...] for i in range(n_en)]
    pos += n_en
    accs = [jnp.dot(x, w, preferred_element_type=_F32) for x, w in zip(xs, ws)]
    outs = epilogue(accs, emn, en)
    for o_ref, o in zip(refs[pos + n_skip:], outs):
        o_ref[...] = o.astype(o_ref.dtype)


def _mm_tiles(m, n, x_row_bytes, k_total, n_emn, out_bytes, m_unit):
    best = None
    for tm in (1024, 512, 256, 128, 64, 32, 16, 8):
        if tm > m or m_unit % tm:
            continue
        for tn in (1024, 512, 256, 128):
            if tn > n or n % tn:
                continue
            need = 2 * (tm * x_row_bytes + k_total * tn * 2 + tm * tn * (4 * n_emn + out_bytes)) + tm * tn * 4
            if need > _VMEM_BUDGET:
                continue
            score = (tm * tn / (tm + tn), tm)
            if best is None or score > best[0]:
                best = (score, tm, tn)
    assert best is not None, (m, n)
    return best[1], best[2]


def _mm(xs, ws, *, epilogue, out_dtypes, extras_mn=(), extras_n=(), name="mm", stack=None):
    m = xs[0].shape[0]
    n = ws[0].shape[1]
    x_row_bytes = sum(a.shape[1] * a.dtype.itemsize for a in xs)
    k_total = sum(w.shape[0] for w in ws)
    out_bytes = sum(jnp.dtype(dt).itemsize for dt in out_dtypes)
    tm, tn = _mm_tiles(m, n, x_row_bytes, k_total, len(extras_mn), out_bytes, m if stack is None else stack[2])
    in_specs = [pl.BlockSpec((tm, a.shape[1]), lambda i, j: (i, 0)) for a in xs]
    in_specs += [pl.BlockSpec((w.shape[0], tn), lambda i, j: (0, j)) for w in ws]
    in_specs += [pl.BlockSpec((tm, tn), lambda i, j: (i, j)) for _ in extras_mn]
    in_specs += [pl.BlockSpec((1, tn), lambda i, j: (0, j)) for _ in extras_n]
    kern = functools.partial(_mm_kernel, n_w=len(ws), n_emn=len(extras_mn), n_en=len(extras_n),
                             n_skip=0 if stack is None else 1, epilogue=epilogue)
    out_specs = [pl.BlockSpec((tm, tn), lambda i, j: (i, j)) for _ in out_dtypes]
    out_shape = [jax.ShapeDtypeStruct((m, n), dt) for dt in out_dtypes]
    operands = [*xs, *ws, *extras_mn, *[e.reshape(1, n) for e in extras_n]]
    aliases = {}
    if stack is not None:
        buf, slot, seq = stack
        per = seq // tm
        n_slots = buf.shape[0] // m
        out_specs[0] = pl.BlockSpec((tm, tn), lambda i, j: ((i // per * n_slots + slot) * per + i % per, j))
        out_shape[0] = jax.ShapeDtypeStruct(buf.shape, buf.dtype)
        in_specs.append(pl.BlockSpec(memory_space=pl.ANY))
        aliases = {len(operands): 0}
        operands.append(buf)
    return pl.pallas_call(
        kern,
        grid=(m // tm, n // tn),
        in_specs=in_specs,
        out_specs=out_specs,
        out_shape=out_shape,
        input_output_aliases=aliases,
        compiler_params=_params("arbitrary", "arbitrary"),
        name=name,
    )(*operands)


def _lora_kernel(x_ref, w1_ref, w2_ref, *refs, act, n_en, epilogue):
    h = jnp.dot(x_ref[...], w1_ref[...], preferred_element_type=_F32)
    acc = jnp.dot(act(h).astype(_BF16), w2_ref[...], preferred_element_type=_F32)
    en = [refs[i][...] for i in range(n_en)]
    refs[n_en][...] = epilogue(acc, en).astype(refs[n_en].dtype)


def _lora(x, w1, w2, *, act, epilogue, extras_n=(), name="lora"):
    m, k = x.shape
    r = w1.shape[1]
    n = w2.shape[1]
    rp = -(-r // _LANES) * _LANES
    w1 = jnp.pad(w1, ((0, 0), (0, rp - r))).astype(_BF16)
    w2 = jnp.pad(w2, ((0, rp - r), (0, 0))).astype(_BF16)
    tm = _divisor_tile(m, 512, 16)
    kern = functools.partial(_lora_kernel, act=act, n_en=len(extras_n), epilogue=epilogue)
    return pl.pallas_call(
        kern,
        grid=(m // tm,),
        in_specs=[pl.BlockSpec((tm, k), lambda i: (i, 0)),
                  pl.BlockSpec((k, rp), lambda i: (0, 0)),
                  pl.BlockSpec((rp, n), lambda i: (0, 0))]
                 + [pl.BlockSpec((1, n), lambda i: (0, 0)) for _ in extras_n],
        out_specs=pl.BlockSpec((tm, n), lambda i: (i, 0)),
        out_shape=jax.ShapeDtypeStruct((m, n), _F32),
        compiler_params=_params("arbitrary"),
        name=name,
    )(x, w1, w2, *[e.reshape(1, n) for e in extras_n])


def _scan_kernel(r_ref, w_ref, k_ref, v_ref, a_ref, kkp_ref, kap_ref, rk_ref, gw_ref, gb_ref, s0_ref,
                 o_ref, s_ref, kk_s, kka_s, km_s, rr_s, gam_s, *, tb, n):
    @pl.when(pl.program_id(1) == 0)
    def _():
        s_ref[...] = s0_ref[...]

    gam_s[...] = jnp.ones(gam_s.shape, _F32)

    def step(t, carry):
        k_t = k_ref[t, 0:n, :]
        a_t = a_ref[t, 0:n, :]
        r_t = r_ref[t, 0:n, :]
        kk = k_t * kkp_ref[...]
        kk = kk * lax.rsqrt(jnp.maximum(jnp.sum(kk * kk, axis=0, keepdims=True), _KK_EPS))
        km = k_t * (1.0 + (a_t - 1.0) * kap_ref[...])
        gam_prev = gam_s[...]
        gam = gam_prev * w_ref[t, 0:n, :]
        inv = 1.0 / gam
        gam_s[...] = gam
        kk_s[...] = kk * gam_prev
        kka_s[...] = kk * a_t * inv
        km_s[...] = km * inv
        rr_s[...] = r_t * gam
        v_t = v_ref[t, 0:n, :]
        sk = [jnp.zeros_like(v_t), jnp.zeros_like(v_t)]
        for j in range(n):
            sk[j % 2] = sk[j % 2] + s_ref[j] * kk_s[j:j + 1, :]
        sk = sk[0] + sk[1]
        o = [jnp.zeros_like(v_t), jnp.zeros_like(v_t)]
        for j in range(n):
            s_new = s_ref[j] + (v_t * km_s[j:j + 1, :] - sk * kka_s[j:j + 1, :])
            s_ref[j] = s_new
            o[j % 2] = o[j % 2] + s_new * rr_s[j:j + 1, :]
        o = o[0] + o[1]
        mean = jnp.mean(o, axis=0, keepdims=True)
        var = jnp.mean(jnp.square(o - mean), axis=0, keepdims=True)
        o = (o - mean) * lax.rsqrt(var + _GN_EPS) * gw_ref[...] + gb_ref[...]
        bonus = jnp.sum(r_t * km * rk_ref[...], axis=0, keepdims=True) * v_t
        o_ref[t, 0:n, :] = o + bonus
        o_ref[t, n:, :] = jnp.zeros((o_ref.shape[1] - n, _LANES), _F32)
        return carry

    lax.fori_loop(0, tb, step, 0)
    for j in range(n):
        s_ref[j] = s_ref[j] * gam_s[j:j + 1, :]


def _rwkv_scan(seqs, params, s0, n):
    l, rows, p = seqs[0].shape
    tb = _divisor_tile(l, _SCAN_STEPS, 1)
    seq_spec = pl.BlockSpec((tb, rows, _LANES), lambda c, i: (i, 0, c))
    par_spec = pl.BlockSpec((n, _LANES), lambda c, i: (0, c))
    st_spec = pl.BlockSpec((n, n, _LANES), lambda c, i: (0, 0, c))
    return pl.pallas_call(
        functools.partial(_scan_kernel, tb=tb, n=n),
        grid=(p // _LANES, l // tb),
        in_specs=[seq_spec] * 5 + [par_spec] * 5 + [st_spec],
        out_specs=[seq_spec, st_spec],
        out_shape=[jax.ShapeDtypeStruct((l, rows, p), _F32), jax.ShapeDtypeStruct((n, n, p), _F32)],
        scratch_shapes=[pltpu.VMEM((n, _LANES), _F32)] * 5,
        compiler_params=_params("arbitrary", "arbitrary"),
        name="rwkv_scan",
    )(*seqs, *params, s0)


def _lanes_in_pallas(b, l, h, n):
    return l % _LANES == 0 and _LANES % h == 0 and b % (_LANES // h) == 0 and _LANES % n == 0


def _to_lanes_kernel(x_ref, o_ref, y_s, *, n):
    g, steps, d = x_ref.shape
    tiles = d // _LANES
    per = _LANES // n
    pitch = o_ref.shape[0] // steps
    for b in range(g):
        for c in range(tiles):
            t = x_ref[b, :, c * _LANES:(c + 1) * _LANES].T
            for hh in range(per):
                r0 = ((b * tiles + c) * per + hh) * pitch
                y_s[r0:r0 + n, :] = t[hh * n:(hh + 1) * n, :]
    for j in range(n):
        o_ref[pl.ds(j, steps, stride=pitch), :] = y_s[pl.ds(j, _LANES, stride=pitch), :].T
    for j in range(n, pitch):
        o_ref[pl.ds(j, steps, stride=pitch), :] = jnp.zeros((steps, _LANES), _F32)


def _from_lanes_kernel(o_ref, g_ref, out_ref, y_s, *, n):
    g, steps, d = g_ref.shape
    tiles = d // _LANES
    per = _LANES // n
    pitch = o_ref.shape[0] // steps
    for j in range(n):
        y_s[pl.ds(j, _LANES, stride=pitch), :] = o_ref[pl.ds(j, steps, stride=pitch), :].T
    for b in range(g):
        for c in range(tiles):
            t = jnp.concatenate([y_s[((b * tiles + c) * per + hh) * pitch:((b * tiles + c) * per + hh) * pitch + n, :]
                                 for hh in range(per)], axis=0)
            cols = slice(c * _LANES, (c + 1) * _LANES)
            out_ref[b, :, cols] = (t.T * g_ref[b, :, cols]).astype(out_ref.dtype)


def _to_lanes(x, b, l, h, n, p):
    d = h * n
    rows = n + _ROW_PAD
    if not _lanes_in_pallas(b, l, h, n):
        y = x.reshape(b, l, h, n).transpose(1, 3, 0, 2).reshape(l, n, b * h)
        return jnp.pad(y, ((0, 0), (0, _ROW_PAD), (0, p - b * h)))
    g = _LANES // h
    out = pl.pallas_call(
        functools.partial(_to_lanes_kernel, n=n),
        grid=(p // _LANES, l // _LANES),
        in_specs=[pl.BlockSpec((g, _LANES, d), lambda c, i: (c, i, 0))],
        out_specs=pl.BlockSpec((_LANES * rows, _LANES), lambda c, i: (i, c)),
        out_shape=jax.ShapeDtypeStruct((l * rows, p), _F32),
        scratch_shapes=[pltpu.VMEM((_LANES * rows, _LANES), _F32)],
        compiler_params=_params("arbitrary", "arbitrary"),
        name="to_lanes",
    )(x.reshape(b, l, d))
    return out.reshape(l, rows, p)


def _from_lanes(o, gate, b, l, h, n):
    d = h * n
    rows, p = o.shape[1:]
    if not _lanes_in_pallas(b, l, h, n):
        y = o[:, :n, :b * h].reshape(l, n, b, h).transpose(2, 0, 3, 1).reshape(b * l, d)
        return (y * gate).astype(_BF16)
    g = _LANES // h
    out = pl.pallas_call(
        functools.partial(_from_lanes_kernel, n=n),
        grid=(p // _LANES, l // _LANES),
        in_specs=[pl.BlockSpec((_LANES * rows, _LANES), lambda c, i: (i, c)),
                  pl.BlockSpec((g, _LANES, d), lambda c, i: (c, i, 0))],
        out_specs=pl.BlockSpec((g, _LANES, d), lambda c, i: (c, i, 0)),
        out_shape=jax.ShapeDtypeStruct((b, l, d), _BF16),
        scratch_shapes=[pltpu.VMEM((_LANES * rows, _LANES), _F32)],
        compiler_params=_params("arbitrary", "arbitrary"),
        name="from_lanes",
    )(o.reshape(l * rows, p), gate.reshape(b, l, d))
    return out.reshape(b * l, d)


def _param_lanes(w, b, h, n, p):
    y = jnp.broadcast_to(w.reshape(h, n).T[:, None, :], (n, b, h)).reshape(n, b * h)
    return jnp.pad(y, ((0, 0), (0, p - b * h)))


def _cumsum_kernel(x_ref, o_ref, *, scale):
    x = x_ref[...] * scale
    length = x.shape[1]
    col = lax.broadcasted_iota(jnp.int32, x.shape, 1)
    if length % _LANES == 0:
        s = 1
        while s < length:
            x = x + jnp.where(col >= s, pltpu.roll(x, s, axis=1), 0.0)
            s *= 2
        o_ref[...] = x
    else:
        acc = jnp.zeros_like(x)
        for s in range(length):
            acc = acc + jnp.where(col >= s, x[:, s:s + 1], 0.0)
        o_ref[...] = acc


def _cumsum_rows(x, scale):
    return pl.pallas_call(
        functools.partial(_cumsum_kernel, scale=scale),
        out_shape=jax.ShapeDtypeStruct(x.shape, _F32),
        compiler_params=pltpu.CompilerParams(vmem_limit_bytes=_VMEM_LIMIT),
        name="logf_cumsum",
    )(x)


def _fox_kernel(q_ref, k_ref, v_ref, c_ref, o_ref, *, tq, dh, n_grp):
    i = pl.program_id(2)
    row = lax.broadcasted_iota(jnp.int32, (tq, tq), 0)
    col = lax.broadcasted_iota(jnp.int32, (tq, tq), 1)
    q0 = pl.multiple_of(i * tq, tq)
    qs = [q_ref[0, :, g * dh:(g + 1) * dh] for g in range(n_grp)]
    cqs = [jnp.sum(jnp.where(row == col, c_ref[g, :, pl.ds(q0, tq)], 0.0), axis=1, keepdims=True)
           for g in range(n_grp)]

    def block(j, carry, masked):
        k0 = pl.multiple_of(j * tq, tq)
        out = []
        for g in range(n_grp):
            m, l, acc = carry[g]
            lanes = slice(g * dh, (g + 1) * dh)
            s = lax.dot_general(qs[g], k_ref[0, pl.ds(k0, tq), lanes], _NT, preferred_element_type=_F32)
            u = s - c_ref[g, :, pl.ds(k0, tq)]
            if masked:
                u = jnp.where(col <= row, u, _NEG)
            m_new = jnp.maximum(m, jnp.max(u, axis=1, keepdims=True) + cqs[g])
            alpha = jnp.exp2(m - m_new)
            p = jnp.exp2(u - (m_new - cqs[g]))
            l = alpha * l + jnp.sum(p, axis=1, keepdims=True)
            acc = alpha * acc + jnp.dot(p.astype(_BF16), v_ref[0, pl.ds(k0, tq), lanes],
                                        preferred_element_type=_F32)
            out.append((m_new, l, acc))
        return tuple(out)

    init = tuple((jnp.full((tq, 1), _NEG, _F32), jnp.zeros((tq, 1), _F32), jnp.zeros((tq, dh), _F32))
                 for _ in range(n_grp))
    carry = lax.fori_loop(0, i, lambda j, c: block(j, c, False), init)
    carry = block(i, carry, True)
    for g in range(n_grp):
        _, l, acc = carry[g]
        o_ref[0, :, g * dh:(g + 1) * dh] = (acc / l).astype(o_ref.dtype)


def _fox_attention(q, k, v, c_rows, b, l, h, dh):
    tq = _divisor_tile(l, 512, _LANES)
    n_grp = 2 if h % 2 == 0 else 1
    qo_spec = pl.BlockSpec((1, tq, n_grp * dh), lambda bi, hi, i: (bi, i, hi))
    kv_spec = pl.BlockSpec((1, l, n_grp * dh), lambda bi, hi, i: (bi, 0, hi))
    c_spec = pl.BlockSpec((n_grp, 1, l), lambda bi, hi, i: (bi * (h // n_grp) + hi, 0, 0))
    return pl.pallas_call(
        functools.partial(_fox_kernel, tq=tq, dh=dh, n_grp=n_grp),
        grid=(b, h // n_grp, l // tq),
        in_specs=[qo_spec, kv_spec, kv_spec, c_spec],
        out_specs=qo_spec,
        out_shape=jax.ShapeDtypeStruct((b, l, h * dh), _BF16),
        compiler_params=_params("arbitrary", "arbitrary", "arbitrary"),
        name="fox_attention",
    )(q, k, v, c_rows)


def _suffix_kernel(x_ref, sfx_ref, tot_ref, *, stride):
    x = x_ref[...]
    n = x.shape[1]
    col = lax.broadcasted_iota(jnp.int32, x.shape, 1)
    inc = x
    tot = x
    s = stride
    while s < n:
        inc = inc + jnp.where(col + s < n, pltpu.roll(inc, n - s, axis=1), 0.0)
        tot = tot + pltpu.roll(tot, n - s, axis=1)
        s *= 2
    sfx_ref[...] = inc - x
    tot_ref[...] = tot


def _page_suffix(cache_logf):
    n_phys, n_fox, ps, h = cache_logf.shape
    assert ps & (ps - 1) == 0 and (ps * h) % _LANES == 0, (ps, h)
    rows = n_phys * n_fox
    tr = _divisor_tile(rows, 256, 8)
    spec = pl.BlockSpec((tr, ps * h), lambda i: (i, 0))
    sfx, tot = pl.pallas_call(
        functools.partial(_suffix_kernel, stride=h),
        grid=(rows // tr,),
        in_specs=[spec],
        out_specs=[spec, spec],
        out_shape=[jax.ShapeDtypeStruct((rows, ps * h), _F32)] * 2,
        compiler_params=_params("arbitrary"),
        name="page_suffix",
    )(cache_logf.reshape(rows, ps * h))
    return sfx, tot


def _decode_kernel(pt_ref, q_ref, cq_ref, cn_ref, kn_ref, vn_ref, mask_ref, *refs, n_heads, n_q, dh, n_grp, n_steps,
                   n_pages, n_layers, layer):
    kps, vps = refs[:n_grp], refs[n_grp:2 * n_grp]
    sfs, tts = refs[2 * n_grp:3 * n_grp], refs[3 * n_grp:4 * n_grp]
    o_ref, m_s, l_s, acc_s, carry_s = refs[4 * n_grp:]
    p = pl.program_id(1)
    q = q_ref[0]
    cq = cq_ref[0]

    def update(us, vs):
        m_old = m_s[...]
        mu = functools.reduce(jnp.maximum, [jnp.max(u, axis=1, keepdims=True) for u in us])
        m_new = jnp.maximum(m_old, mu + cq)
        shift = m_new - cq
        alpha = jnp.exp(m_old - m_new)
        l = alpha * l_s[...]
        acc = alpha * acc_s[...]
        for u, v in zip(us, vs):
            pr = jnp.exp(u - shift)
            l = l + jnp.sum(pr, axis=1, keepdims=True)
            acc = acc + jnp.dot(pr.astype(_BF16), v, preferred_element_type=_F32)
        m_s[...] = m_new
        l_s[...] = l
        acc_s[...] = acc

    @pl.when(p == 0)
    def _():
        m_s[...] = jnp.full(m_s.shape, _NEG, _F32)
        l_s[...] = jnp.zeros(l_s.shape, _F32)
        acc_s[...] = jnp.zeros(acc_s.shape, _F32)
        carry_s[...] = jnp.zeros(carry_s.shape, _F32)
        s = lax.dot_general(q, kn_ref[0], _NT, preferred_element_type=_F32)
        row = lax.broadcasted_iota(jnp.int32, s.shape, 0)
        col = lax.broadcasted_iota(jnp.int32, s.shape, 1)
        valid = (lax.rem(col, n_heads) == row // n_q) & (col // n_heads <= lax.rem(row, n_q))
        update([jnp.where(valid, s - cn_ref[0], _NEG)], [vn_ref[0]])

    @pl.when(p > 0)
    def _():
        carry = carry_s[...]
        us, vs = [], []
        for g in range(n_grp):
            page = pt_ref[pl.program_id(0), n_pages - 1 - (p - 1) * n_grp - g]
            row = pl.ds(lax.rem(page * n_layers + layer, _SUBLANES), 1)
            s = lax.dot_general(q, kps[g][...].astype(_BF16), _NT, preferred_element_type=_F32)
            us.append(s + ((sfs[g][row, :] + carry) + mask_ref[...]))
            carry = carry + tts[g][row, :]
            vs.append(vps[g][...].astype(_BF16))
        carry_s[...] = carry
        update(us, vs)

    @pl.when(p == n_steps - 1)
    def _():
        acc = acc_s[...] / l_s[...]
        for hh in range(n_heads):
            o_ref[0, :, hh * dh:(hh + 1) * dh] = acc[hh * n_q:(hh + 1) * n_q, :]


def _fox_decode(q, k_new, v_new, c_new, cache_k, cache_v, sfx, tot, page_table, layer):
    b, nq, h, dh = q.shape
    n_phys, n_fox, ps = cache_k.shape[:3]
    n_pages = page_table.shape[1]
    r = h * nq
    n_grp = next(g for g in (4, 2, 1) if n_pages % g == 0)
    n_steps = n_pages // n_grp + 1
    qs = (q * dh ** -0.5).transpose(0, 2, 1, 3).reshape(b, r, dh).astype(_BF16)
    cq = c_new.transpose(0, 2, 1).reshape(b, r, 1)
    row = lax.broadcasted_iota(jnp.int32, (r, ps * h), 0)
    col = lax.broadcasted_iota(jnp.int32, (r, ps * h), 1)
    mask = jnp.where(col % h == row // nq, 0.0, _NEG).astype(_F32)

    page_of = lambda bi, p, pt, g: pt[bi, n_pages - 1 - (jnp.maximum(p, 1) - 1) * n_grp - g]

    def page_spec(g):
        return pl.BlockSpec((None, None, ps * h, dh), lambda bi, p, pt: (page_of(bi, p, pt, g), layer, 0, 0))

    def group_spec(g):
        return pl.BlockSpec((_SUBLANES, ps * h),
                            lambda bi, p, pt: ((page_of(bi, p, pt, g) * n_fox + layer) // _SUBLANES, 0))

    per_seq = lambda shape: pl.BlockSpec((1,) + shape, lambda bi, p, pt: (bi, 0, 0))
    grid_spec = pltpu.PrefetchScalarGridSpec(
        num_scalar_prefetch=1,
        grid=(b, n_steps),
        in_specs=[per_seq((r, dh)), per_seq((r, 1)), per_seq((1, nq * h)), per_seq((nq * h, dh)),
                  per_seq((nq * h, dh)), pl.BlockSpec((r, ps * h), lambda bi, p, pt: (0, 0))]
                 + [page_spec(g) for g in range(n_grp)] * 2
                 + [group_spec(g) for g in range(n_grp)] * 2,
        out_specs=per_seq((nq, h * dh)),
        scratch_shapes=[pltpu.VMEM((r, 1), _F32), pltpu.VMEM((r, 1), _F32), pltpu.VMEM((r, dh), _F32),
                        pltpu.VMEM((1, ps * h), _F32)],
    )
    kc = cache_k.reshape(n_phys, n_fox, ps * h, dh)
    vc = cache_v.reshape(n_phys, n_fox, ps * h, dh)
    out = pl.pallas_call(
        functools.partial(_decode_kernel, n_heads=h, n_q=nq, dh=dh, n_grp=n_grp, n_steps=n_steps,
                          n_pages=n_pages, n_layers=n_fox, layer=layer),
        grid_spec=grid_spec,
        out_shape=jax.ShapeDtypeStruct((b, nq, h * dh), _F32),
        compiler_params=_params("arbitrary", "arbitrary"),
        name="fox_decode",
    )(page_table, qs, cq, c_new.reshape(b, 1, nq * h), k_new.reshape(b, nq * h, dh).astype(_BF16),
      v_new.reshape(b, nq * h, dh).astype(_BF16), mask, *([kc] * n_grp), *([vc] * n_grp),
      *([sfx] * n_grp), *([tot] * n_grp))
    return out.reshape(b * nq, h * dh).astype(_BF16)


def _first(accs, emn, en):
    return [accs[0]]


def _rwkv_layer(x, shift_prev, wkv_prev, v_first, j, i, W, Wb):
    b, l, d = x.shape
    h, n = wkv_prev.shape[1], wkv_prev.shape[2]
    xr, xw, xk, xv, xa, xg = _rwkv_mix(x, shift_prev, W['norm_mix'][i], W['rwkv_mu'][j])
    (r,) = _mm([xr], [Wb['rwkv_w_r'][j]], epilogue=_first, out_dtypes=[_F32], name="rwkv_r")
    decay = _lora(xw, W['rwkv_w1'][j], W['rwkv_w2'][j], act=jnp.tanh, extras_n=[W['rwkv_w0'][j]],
                  epilogue=lambda acc, en: jnp.exp(-jnp.exp(-_softplus(-(en[0] + acc)) - 0.5)), name="rwkv_decay")
    a = _lora(xa, W['rwkv_a1'][j], W['rwkv_a2'][j], act=lambda t: t, extras_n=[W['rwkv_a0'][j]],
              epilogue=lambda acc, en: _sigmoid(en[0] + acc), name="rwkv_a")
    g = _lora(xg, W['rwkv_g1'][j], W['rwkv_g2'][j], act=_sigmoid, epilogue=lambda acc, en: acc, name="rwkv_g")
    if j == 0:
        (v,) = _mm([xv], [Wb['rwkv_w_v'][j]], epilogue=_first, out_dtypes=[_F32], name="rwkv_v")
        v_first = v
    else:
        gate = _lora(xv, W['rwkv_v1'][j - 1], W['rwkv_v2'][j - 1], act=lambda t: t, extras_n=[W['rwkv_v0'][j - 1]],
                     epilogue=lambda acc, en: _sigmoid(en[0] + acc), name="rwkv_vgate")
        (v,) = _mm([xv], [Wb['rwkv_w_v'][j]], extras_mn=[v_first, gate], out_dtypes=[_F32], name="rwkv_v",
                   epilogue=lambda accs, emn, en: [accs[0] + (emn[0] - accs[0]) * emn[1]])
    (k,) = _mm([xk], [Wb['rwkv_w_k'][j]], epilogue=_first, out_dtypes=[_F32], name="rwkv_k")
    p = -(-(b * h) // _LANES) * _LANES
    seqs = [_to_lanes(t, b, l, h, n, p) for t in (r, decay, k, v, a)]
    params = [_param_lanes(w.reshape(-1), b, h, n, p) for w in
              (W['rwkv_k_k'][j], W['rwkv_k_a'][j], W['rwkv_r_k'][j], W['rwkv_gn_w'][j], W['rwkv_gn_b'][j])]
    s0 = jnp.pad(wkv_prev.astype(_F32).transpose(3, 2, 0, 1).reshape(n, n, b * h), ((0, 0), (0, 0), (0, p - b * h)))
    o, s_last = _rwkv_scan(seqs, params, s0, n)
    og = _from_lanes(o, g, b, l, h, n)
    s_last = s_last[:, :, :b * h].reshape(n, n, b, h).transpose(2, 3, 1, 0)
    x2 = x.reshape(b * l, d)
    (x2,) = _mm([og], [Wb['rwkv_w_o'][j]], extras_mn=[x2], out_dtypes=[_F32], name="rwkv_out",
                epilogue=lambda accs, emn, en: [emn[0] + accs[0]])
    shift_out = _rmsnorm(x[:, -1, :], W['norm_mix'][i], _F32)
    return x2.reshape(b, l, d), shift_out, s_last, v_first


def _fox_layer(x, j, i, W, Wb, paged, kv_bufs):
    b, l, d = x.shape
    h = W['fox_b_f'].shape[1]
    dh = d // h
    x2 = x.reshape(b * l, d)
    hn = _rmsnorm(x2, W['norm_mix'][i], _BF16)
    w_q, w_k, w_v, w_f = Wb['fox_in'][j]
    if paged is None:
        q_scale = dh ** -0.5 * _LOG2E
        (q,) = _mm([hn], [w_q], epilogue=lambda accs, emn, en: [accs[0] * q_scale], out_dtypes=[_BF16], name="fox_q")
    else:
        (q,) = _mm([hn], [w_q], epilogue=_first, out_dtypes=[_F32], name="fox_q")
    twice = lambda accs, emn, en: [accs[0], accs[0]]
    stack_k, stack_v = (None, None) if kv_bufs is None else ((kv_bufs[0], j, l), (kv_bufs[1], j, l))
    k, kb = _mm([hn], [w_k], epilogue=twice, out_dtypes=[_F32, _BF16], name="fox_k", stack=stack_k)
    v, vb = _mm([hn], [w_v], epilogue=twice, out_dtypes=[_F32, _BF16], name="fox_v", stack=stack_v)
    bias_f = jnp.pad(W['fox_b_f'][j], (0, w_f.shape[1] - h))
    (logf_pad,) = _mm([hn], [w_f], extras_n=[bias_f], out_dtypes=[_F32], name="fox_logf",
                      epilogue=lambda accs, emn, en: [-_softplus(-(accs[0] + en[0]))])
    logf = logf_pad[:, :h].reshape(b, l, h)
    c_rows = _cumsum_rows(logf.transpose(0, 2, 1).reshape(b * h, l), _LOG2E if paged is None else 1.0)
    if paged is None:
        o = _fox_attention(q.reshape(b, l, d), kb.reshape(b, l, d), vb.reshape(b, l, d),
                           c_rows.reshape(b * h, 1, l), b, l, h, dh)
        o = o.reshape(b * l, d)
    else:
        cache_k, cache_v, sfx, tot, page_table = paged
        c_new = c_rows.reshape(b, h, l).transpose(0, 2, 1)
        o = _fox_decode(q.reshape(b, l, h, dh), k.reshape(b, l, h, dh), v.reshape(b, l, h, dh), c_new,
                        cache_k, cache_v, sfx, tot, page_table, j)
    (x2,) = _mm([o], [Wb['fox_w_o'][j]], extras_mn=[x2], out_dtypes=[_F32], name="fox_out",
                epilogue=lambda accs, emn, en: [emn[0] + accs[0]])
    return x2.reshape(b, l, d), k, v, logf


def _ffn_ple(x, p_i, i, W, Wb):
    b, l, d = x.shape
    x2 = x.reshape(b * l, d)
    hn = _rmsnorm(x2, W['norm_ffn'][i], _BF16)
    (up,) = _mm([hn], [Wb['w_up'][i]], out_dtypes=[_BF16], name="ffn_up",
                epilogue=lambda accs, emn, en: [jnp.square(jnp.maximum(accs[0], 0.0))])
    (x2,) = _mm([up], [Wb['w_down'][i]], extras_mn=[x2], out_dtypes=[_F32], name="ffn_down",
                epilogue=lambda accs, emn, en: [emn[0] + accs[0]])
    hp = _rmsnorm(x2, W['norm_ple'][i], _BF16)
    pb = p_i.reshape(b * l, -1).astype(_BF16)
    (x2,) = _mm([hp, pb], [Wb['w_ple_gate'][i], Wb['w_ple'][i]], extras_mn=[x2], out_dtypes=[_F32], name="ple",
                epilogue=lambda accs, emn, en: [emn[0] + _sigmoid(accs[0]) * accs[1]])
    return x2.reshape(b, l, d)


def _trunk(x, p, shift_in, wkv_in, paged, W, Wb):
    depth = W['norm_mix'].shape[0]
    b, l, d = x.shape
    n_fox = depth // 2
    h = W['fox_b_f'].shape[1]
    kv_bufs = None
    if paged is None and l % _LANES == 0:
        kv_bufs = [jnp.zeros((b * n_fox * l, d), _F32), jnp.zeros((b * n_fox * l, d), _F32)]
    v_first = None
    shifts, wkvs, ks, vs, lfs = [], [], [], [], []
    for i in range(depth):
        j = i // 2
        if i % 2 == 0:
            x, sh, s_last, v_first = _rwkv_layer(x, shift_in[j], wkv_in[j], v_first, j, i, W, Wb)
            shifts.append(sh)
            wkvs.append(s_last)
        else:
            x, k, v, lf = _fox_layer(x, j, i, W, Wb, paged, kv_bufs)
            if kv_bufs is not None:
                kv_bufs = [k, v]
            else:
                ks.append(k.reshape(b, l, h, d // h))
                vs.append(v.reshape(b, l, h, d // h))
            lfs.append(lf)
        x = _ffn_ple(x, p[i], i, W, Wb)
    y = _rmsnorm(x.reshape(b * l, d), W['norm_out'], _F32).reshape(b, l, d)
    if kv_bufs is None:
        k_all, v_all = jnp.stack(ks, 1), jnp.stack(vs, 1)
    else:
        k_all, v_all = (t.reshape(b, n_fox, l, h, d // h) for t in kv_bufs)
    return y, jnp.stack(shifts, 0), jnp.stack(wkvs, 0), k_all, v_all, jnp.stack(lfs, 1)


def kernel(x_prompt, x_sample, cache_k, cache_v, cache_logf, state_wkv, state_shift, page_table, p_prompt, p_sample, norm_mix, norm_ffn, norm_ple, norm_out, rwkv_mu, rwkv_w_r, rwkv_w_k, rwkv_w_v, rwkv_w_o, rwkv_w0, rwkv_w1, rwkv_w2, rwkv_a0, rwkv_a1, rwkv_a2, rwkv_v0, rwkv_v1, rwkv_v2, rwkv_g1, rwkv_g2, rwkv_k_k, rwkv_k_a, rwkv_r_k, rwkv_gn_w, rwkv_gn_b, fox_w_in, fox_b_f, fox_w_o, w_up, w_down, w_ple, w_ple_gate):
    W = {
        'norm_mix': norm_mix, 'norm_ffn': norm_ffn, 'norm_ple': norm_ple, 'norm_out': norm_out,
        'rwkv_mu': rwkv_mu, 'rwkv_w0': rwkv_w0, 'rwkv_w1': rwkv_w1, 'rwkv_w2': rwkv_w2,
        'rwkv_a0': rwkv_a0, 'rwkv_a1': rwkv_a1, 'rwkv_a2': rwkv_a2,
        'rwkv_v0': rwkv_v0, 'rwkv_v1': rwkv_v1, 'rwkv_v2': rwkv_v2,
        'rwkv_g1': rwkv_g1, 'rwkv_g2': rwkv_g2, 'rwkv_k_k': rwkv_k_k, 'rwkv_k_a': rwkv_k_a,
        'rwkv_r_k': rwkv_r_k, 'rwkv_gn_w': rwkv_gn_w, 'rwkv_gn_b': rwkv_gn_b, 'fox_b_f': fox_b_f,
    }
    d = x_prompt.shape[-1]
    n_fox_heads = fox_b_f.shape[1]
    f_pad = -(-n_fox_heads // _LANES) * _LANES - n_fox_heads
    bf = lambda w: w.astype(_BF16)
    Wb = {
        'rwkv_w_r': bf(rwkv_w_r), 'rwkv_w_k': bf(rwkv_w_k), 'rwkv_w_v': bf(rwkv_w_v), 'rwkv_w_o': bf(rwkv_w_o),
        'fox_in': [(bf(w[:, :d]), bf(w[:, d:2 * d]), bf(w[:, 2 * d:3 * d]),
                    bf(jnp.pad(w[:, 3 * d:], ((0, 0), (0, f_pad))))) for w in fox_w_in],
        'fox_w_o': bf(fox_w_o), 'w_up': bf(w_up), 'w_down': bf(w_down),
        'w_ple': bf(w_ple), 'w_ple_gate': bf(w_ple_gate),
    }
    n_rwkv, _, n_heads, n_dim, _ = state_wkv.shape
    b = x_prompt.shape[0]
    shift0 = jnp.zeros((n_rwkv, b, d), x_prompt.dtype)
    wkv0 = jnp.zeros((n_rwkv, b, n_heads, n_dim, n_dim), _F32)
    y_p, shift_p, wkv_p, k_p, v_p, lf_p = _trunk(x_prompt, p_prompt, shift0, wkv0, None, W, Wb)
    sfx, tot = _page_suffix(cache_logf)
    y_s, shift_s, wkv_s, k_s, v_s, lf_s = _trunk(
        x_sample, p_sample, state_shift, state_wkv, (cache_k, cache_v, sfx, tot, page_table), W, Wb)
    return (y_p, y_s, k_p, v_p, lf_p, k_s, v_s, lf_s, wkv_p, shift_p, wkv_s, shift_s)
```

```python
import functools

import jax
import jax.numpy as jnp
from jax import lax
from jax.experimental import pallas as pl
from jax.experimental.pallas import tpu as pltpu

_RMS_EPS = 1e-6
_GN_EPS = 64e-5
_KK_EPS = 1e-24
_NEG = -1e30
_LOG2E = 1.4426950408889634
_LANES = 128
_SUBLANES = 8
_SCAN_STEPS = 64
_ROW_PAD = 8
_VMEM_LIMIT = 56 * 1024 * 1024
_VMEM_BUDGET = 40 * 1024 * 1024
_F32 = jnp.float32
_BF16 = jnp.bfloat16
_NT = (((1,), (1,)), ((), ()))


def _params(*sem):
    return pltpu.CompilerParams(dimension_semantics=sem, vmem_limit_bytes=_VMEM_LIMIT)


def _divisor_tile(n, target, align):
    t = min(n, target)
    t -= t % align
    while t >= align:
        if n % t == 0:
            return t
        t -= align
    return n


def _sigmoid(x):
    return 1.0 / (1.0 + jnp.exp(-x))


def _softplus(x):
    return jnp.maximum(x, 0.0) + jnp.log(1.0 + jnp.exp(-jnp.abs(x)))


def _rms(x, g):
    return x * lax.rsqrt(jnp.mean(x * x, axis=-1, keepdims=True) + _RMS_EPS) * g


def _rmsnorm_kernel(x_ref, g_ref, o_ref):
    o_ref[...] = _rms(x_ref[...], g_ref[...]).astype(o_ref.dtype)


def _rmsnorm(x, g, out_dtype):
    t, d = x.shape
    tr = _divisor_tile(t, 512, 16)
    return pl.pallas_call(
        _rmsnorm_kernel,
        grid=(t // tr,),
        in_specs=[pl.BlockSpec((tr, d), lambda i: (i, 0)), pl.BlockSpec((1, d), lambda i: (0, 0))],
        out_specs=pl.BlockSpec((tr, d), lambda i: (i, 0)),
        out_shape=jax.ShapeDtypeStruct((t, d), out_dtype),
        compiler_params=_params("arbitrary"),
        name="rmsnorm",
    )(x, g.reshape(1, d))


def _mix_kernel(x_ref, xp_ref, sh_ref, g_ref, mu_ref, *o_refs):
    g = g_ref[...]
    xn = _rms(x_ref[0], g)
    prev = _rms(xp_ref[0, 7:8, :], g)
    prev = jnp.where(pl.program_id(1) == 0, sh_ref[0], prev)
    row = lax.broadcasted_iota(jnp.int32, xn.shape, 0)
    xprev = jnp.where(row == 0, prev, pltpu.roll(xn, 1, axis=0))
    xx = xprev - xn
    for c, o_ref in enumerate(o_refs):
        o_ref[0] = (xn + xx * mu_ref[c:c + 1, :]).astype(o_ref.dtype)


def _rwkv_mix(x, shift_prev, g, mu):
    b, l, d = x.shape
    tr = _divisor_tile(l, 256, 8)
    nb8 = tr // 8
    outs = pl.pallas_call(
        _mix_kernel,
        grid=(b, l // tr),
        in_specs=[
            pl.BlockSpec((1, tr, d), lambda bi, i: (bi, i, 0)),
            pl.BlockSpec((1, 8, d), lambda bi, i: (bi, jnp.maximum(i * nb8 - 1, 0), 0)),
            pl.BlockSpec((1, 1, d), lambda bi, i: (bi, 0, 0)),
            pl.BlockSpec((1, d), lambda bi, i: (0, 0)),
            pl.BlockSpec((6, d), lambda bi, i: (0, 0)),
        ],
        out_specs=[pl.BlockSpec((1, tr, d), lambda bi, i: (bi, i, 0))] * 6,
        out_shape=[jax.ShapeDtypeStruct((b, l, d), _BF16)] * 6,
        compiler_params=_params("arbitrary", "arbitrary"),
        name="rwkv_mix",
    )(x, x, shift_prev.reshape(b, 1, d), g.reshape(1, d), mu)
    return [o.reshape(b * l, d) for o in outs]


def _mm_kernel(*refs, n_w, n_emn, n_en, n_skip, split, epilogue):
    xs = [refs[i][...] for i in range(n_w)]
    ws = [refs[n_w + i][...] for i in range(n_w)]
    pos = 2 * n_w
    emn = [refs[pos + i][...] for i in range(n_emn)]
    pos += n_emn
    en = [refs[pos + i][...] for i in range(n_en)]
    pos += n_en
    accs = [jnp.dot(x, w, preferred_element_type=_F32) for x, w in zip(xs, ws)]
    outs = epilogue(accs, emn, en)
    out_refs = refs[pos + n_skip:]
    if split:
        o0 = outs[0].astype(out_refs[0].dtype)
        for g in range(split):
            out_refs[0][pl.ds(g, o0.shape[0], stride=split), :] = o0[:, g * _LANES:(g + 1) * _LANES]
        out_refs, outs = out_refs[1:], outs[1:]
    for o_ref, o in zip(out_refs, outs):
        o_ref[...] = o.astype(o_ref.dtype)


def _mm_tiles(m, n, x_row_bytes, k_total, n_emn, out_bytes, m_unit, n_unit):
    best = None
    for tm in (1024, 512, 256, 128, 64, 32, 16, 8):
        if tm > m or m_unit % tm:
            continue
        for tn in (2048, 1024, 512, 256, 128):
            if tn > n or n % tn or tn % n_unit:
                continue
            need = 2 * (tm * x_row_bytes + k_total * tn * 2 + tm * tn * (4 * n_emn + out_bytes)) + tm * tn * 4
            if need > _VMEM_BUDGET:
                continue
            score = (tm * tn / (tm + tn), tm)
            if best is None or score > best[0]:
                best = (score, tm, tn)
    assert best is not None, (m, n)
    return best[1], best[2]


def _mm(xs, ws, *, epilogue, out_dtypes, extras_mn=(), extras_n=(), name="mm", stack=None, window=None):
    m = xs[0].shape[0]
    col0, n = (0, ws[0].shape[1]) if window is None else window
    x_row_bytes = sum(a.shape[1] * a.dtype.itemsize for a in xs)
    k_total = sum(w.shape[0] for w in ws)
    out_bytes = sum(jnp.dtype(dt).itemsize for dt in out_dtypes)
    tm, tn = _mm_tiles(m, n, x_row_bytes, k_total, len(extras_mn), out_bytes, m if stack is None else stack[2],
                       _LANES if stack is None else n)
    assert col0 % tn == 0, (col0, tn)
    in_specs = [pl.BlockSpec((tm, a.shape[1]), lambda i, j: (i, 0)) for a in xs]
    in_specs += [pl.BlockSpec((w.shape[0], tn), lambda i, j: (0, j + col0 // tn)) for w in ws]
    in_specs += [pl.BlockSpec((tm, tn), lambda i, j: (i, j)) for _ in extras_mn]
    in_specs += [pl.BlockSpec((1, tn), lambda i, j: (0, j)) for _ in extras_n]
    split = 0 if stack is None else n // _LANES
    kern = functools.partial(_mm_kernel, n_w=len(ws), n_emn=len(extras_mn), n_en=len(extras_n),
                             n_skip=0 if stack is None else 1, split=split, epilogue=epilogue)
    out_specs = [pl.BlockSpec((tm, tn), lambda i, j: (i, j)) for _ in out_dtypes]
    out_shape = [jax.ShapeDtypeStruct((m, n), dt) for dt in out_dtypes]
    operands = [*xs, *ws, *extras_mn, *[e.reshape(1, n) for e in extras_n]]
    aliases = {}
    if stack is not None:
        buf, slot, seq = stack
        per = seq // tm
        n_slots = buf.shape[0] // (m * split)
        out_specs[0] = pl.BlockSpec((tm * split, _LANES), lambda i, j: ((i // per * n_slots + slot) * per + i % per, 0))
        out_shape[0] = jax.ShapeDtypeStruct(buf.shape, buf.dtype)
        in_specs.append(pl.BlockSpec(memory_space=pl.ANY))
        aliases = {len(operands): 0}
        operands.append(buf)
    return pl.pallas_call(
        kern,
        grid=(m // tm, n // tn),
        in_specs=in_specs,
        out_specs=out_specs,
        out_shape=out_shape,
        input_output_aliases=aliases,
        compiler_params=_params("arbitrary", "arbitrary"),
        name=name,
    )(*operands)


def _lora_kernel(x_ref, w1_ref, w2_ref, *refs, act, n_en, epilogue):
    h = jnp.dot(x_ref[...], w1_ref[...], preferred_element_type=_F32)
    acc = jnp.dot(act(h).astype(_BF16), w2_ref[...], preferred_element_type=_F32)
    en = [refs[i][...] for i in range(n_en)]
    refs[n_en][...] = epilogue(acc, en).astype(refs[n_en].dtype)


def _lora(x, w1, w2, *, act, epilogue, extras_n=(), name="lora"):
    m, k = x.shape
    r = w1.shape[1]
    n = w2.shape[1]
    rp = -(-r // _LANES) * _LANES
    w1 = jnp.pad(w1, ((0, 0), (0, rp - r))).astype(_BF16)
    w2 = jnp.pad(w2, ((0, rp - r), (0, 0))).astype(_BF16)
    tm = _divisor_tile(m, 512, 16)
    kern = functools.partial(_lora_kernel, act=act, n_en=len(extras_n), epilogue=epilogue)
    return pl.pallas_call(
        kern,
        grid=(m // tm,),
        in_specs=[pl.BlockSpec((tm, k), lambda i: (i, 0)),
                  pl.BlockSpec((k, rp), lambda i: (0, 0)),
                  pl.BlockSpec((rp, n), lambda i: (0, 0))]
                 + [pl.BlockSpec((1, n), lambda i: (0, 0)) for _ in extras_n],
        out_specs=pl.BlockSpec((tm, n), lambda i: (i, 0)),
        out_shape=jax.ShapeDtypeStruct((m, n), _F32),
        compiler_params=_params("arbitrary"),
        name=name,
    )(x, w1, w2, *[e.reshape(1, n) for e in extras_n])


def _scan_kernel(r_ref, w_ref, k_ref, v_ref, a_ref, kkp_ref, kap_ref, rk_ref, gw_ref, gb_ref, s0_ref,
                 o_ref, s_ref, kk_s, kka_s, km_s, rr_s, gam_s, *, tb, n):
    @pl.when(pl.program_id(1) == 0)
    def _():
        s_ref[...] = s0_ref[...]

    gam_s[...] = jnp.ones(gam_s.shape, _F32)

    def step(t, carry):
        k_t = k_ref[t, 0:n, :]
        a_t = a_ref[t, 0:n, :]
        r_t = r_ref[t, 0:n, :]
        kk = k_t * kkp_ref[...]
        kk = kk * lax.rsqrt(jnp.maximum(jnp.sum(kk * kk, axis=0, keepdims=True), _KK_EPS))
        km = k_t * (1.0 + (a_t - 1.0) * kap_ref[...])
        gam_prev = gam_s[...]
        gam = gam_prev * w_ref[t, 0:n, :]
        inv = 1.0 / gam
        gam_s[...] = gam
        kk_s[...] = kk * gam_prev
        kka_s[...] = kk * a_t * inv
        km_s[...] = km * inv
        rr_s[...] = r_t * gam
        v_t = v_ref[t, 0:n, :]
        sk = [jnp.zeros_like(v_t), jnp.zeros_like(v_t)]
        for j in range(n):
            sk[j % 2] = sk[j % 2] + s_ref[j] * kk_s[j:j + 1, :]
        sk = sk[0] + sk[1]
        o = [jnp.zeros_like(v_t), jnp.zeros_like(v_t)]
        for j in range(n):
            s_new = s_ref[j] + (v_t * km_s[j:j + 1, :] - sk * kka_s[j:j + 1, :])
            s_ref[j] = s_new
            o[j % 2] = o[j % 2] + s_new * rr_s[j:j + 1, :]
        o = o[0] + o[1]
        mean = jnp.mean(o, axis=0, keepdims=True)
        var = jnp.mean(jnp.square(o - mean), axis=0, keepdims=True)
        o = (o - mean) * lax.rsqrt(var + _GN_EPS) * gw_ref[...] + gb_ref[...]
        bonus = jnp.sum(r_t * km * rk_ref[...], axis=0, keepdims=True) * v_t
        o_ref[t, 0:n, :] = o + bonus
        o_ref[t, n:, :] = jnp.zeros((o_ref.shape[1] - n, _LANES), _F32)
        return carry

    lax.fori_loop(0, tb, step, 0)
    for j in range(n):
        s_ref[j] = s_ref[j] * gam_s[j:j + 1, :]


def _rwkv_scan(seqs, params, s0, n):
    l, rows, p = seqs[0].shape
    tb = _divisor_tile(l, _SCAN_STEPS, 1)
    seq_spec = pl.BlockSpec((tb, rows, _LANES), lambda c, i: (i, 0, c))
    par_spec = pl.BlockSpec((n, _LANES), lambda c, i: (0, c))
    st_spec = pl.BlockSpec((n, n, _LANES), lambda c, i: (0, 0, c))
    return pl.pallas_call(
        functools.partial(_scan_kernel, tb=tb, n=n),
        grid=(p // _LANES, l // tb),
        in_specs=[seq_spec] * 5 + [par_spec] * 5 + [st_spec],
        out_specs=[seq_spec, st_spec],
        out_shape=[jax.ShapeDtypeStruct((l, rows, p), _F32), jax.ShapeDtypeStruct((n, n, p), _F32)],
        scratch_shapes=[pltpu.VMEM((n, _LANES), _F32)] * 5,
        compiler_params=_params("arbitrary", "arbitrary"),
        name="rwkv_scan",
    )(*seqs, *params, s0)


def _lanes_in_pallas(b, l, h, n):
    return l % _LANES == 0 and _LANES % h == 0 and b % (_LANES // h) == 0 and _LANES % n == 0


def _to_lanes_kernel(x_ref, o_ref, y_s, *, n):
    g, steps, d = x_ref.shape
    tiles = d // _LANES
    per = _LANES // n
    pitch = o_ref.shape[0] // steps
    for b in range(g):
        for c in range(tiles):
            t = x_ref[b, :, c * _LANES:(c + 1) * _LANES].T
            for hh in range(per):
                r0 = ((b * tiles + c) * per + hh) * pitch
                y_s[r0:r0 + n, :] = t[hh * n:(hh + 1) * n, :]
    for j in range(n):
        o_ref[pl.ds(j, steps, stride=pitch), :] = y_s[pl.ds(j, _LANES, stride=pitch), :].T
    for j in range(n, pitch):
        o_ref[pl.ds(j, steps, stride=pitch), :] = jnp.zeros((steps, _LANES), _F32)


def _from_lanes_kernel(o_ref, g_ref, out_ref, y_s, *, n):
    g, steps, d = g_ref.shape
    tiles = d // _LANES
    per = _LANES // n
    pitch = o_ref.shape[0] // steps
    for j in range(n):
        y_s[pl.ds(j, _LANES, stride=pitch), :] = o_ref[pl.ds(j, steps, stride=pitch), :].T
    for b in range(g):
        for c in range(tiles):
            t = jnp.concatenate([y_s[((b * tiles + c) * per + hh) * pitch:((b * tiles + c) * per + hh) * pitch + n, :]
                                 for hh in range(per)], axis=0)
            cols = slice(c * _LANES, (c + 1) * _LANES)
            out_ref[b, :, cols] = (t.T * g_ref[b, :, cols]).astype(out_ref.dtype)


def _to_lanes(x, b, l, h, n, p):
    d = h * n
    rows = n + _ROW_PAD
    if not _lanes_in_pallas(b, l, h, n):
        y = x.reshape(b, l, h, n).transpose(1, 3, 0, 2).reshape(l, n, b * h)
        return jnp.pad(y, ((0, 0), (0, _ROW_PAD), (0, p - b * h)))
    g = _LANES // h
    out = pl.pallas_call(
        functools.partial(_to_lanes_kernel, n=n),
        grid=(p // _LANES, l // _LANES),
        in_specs=[pl.BlockSpec((g, _LANES, d), lambda c, i: (c, i, 0))],
        out_specs=pl.BlockSpec((_LANES * rows, _LANES), lambda c, i: (i, c)),
        out_shape=jax.ShapeDtypeStruct((l * rows, p), _F32),
        scratch_shapes=[pltpu.VMEM((_LANES * rows, _LANES), _F32)],
        compiler_params=_params("arbitrary", "arbitrary"),
        name="to_lanes",
    )(x.reshape(b, l, d))
    return out.reshape(l, rows, p)


def _from_lanes(o, gate, b, l, h, n):
    d = h * n
    rows, p = o.shape[1:]
    if not _lanes_in_pallas(b, l, h, n):
        y = o[:, :n, :b * h].reshape(l, n, b, h).transpose(2, 0, 3, 1).reshape(b * l, d)
        return (y * gate).astype(_BF16)
    g = _LANES // h
    out = pl.pallas_call(
        functools.partial(_from_lanes_kernel, n=n),
        grid=(p // _LANES, l // _LANES),
        in_specs=[pl.BlockSpec((_LANES * rows, _LANES), lambda c, i: (i, c)),
                  pl.BlockSpec((g, _LANES, d), lambda c, i: (c, i, 0))],
        out_specs=pl.BlockSpec((g, _LANES, d), lambda c, i: (c, i, 0)),
        out_shape=jax.ShapeDtypeStruct((b, l, d), _BF16),
        scratch_shapes=[pltpu.VMEM((_LANES * rows, _LANES), _F32)],
        compiler_params=_params("arbitrary", "arbitrary"),
        name="from_lanes",
    )(o.reshape(l * rows, p), gate.reshape(b, l, d))
    return out.reshape(b * l, d)


def _param_lanes(w, b, h, n, p):
    y = jnp.broadcast_to(w.reshape(h, n).T[:, None, :], (n, b, h)).reshape(n, b * h)
    return jnp.pad(y, ((0, 0), (0, p - b * h)))


def _cumsum_kernel(x_ref, o_ref, *, scale):
    x = x_ref[...] * scale
    length = x.shape[1]
    col = lax.broadcasted_iota(jnp.int32, x.shape, 1)
    if length % _LANES == 0:
        s = 1
        while s < length:
            x = x + jnp.where(col >= s, pltpu.roll(x, s, axis=1), 0.0)
            s *= 2
        o_ref[...] = x
    else:
        acc = jnp.zeros_like(x)
        for s in range(length):
            acc = acc + jnp.where(col >= s, x[:, s:s + 1], 0.0)
        o_ref[...] = acc


def _cumsum_rows(x, scale):
    return pl.pallas_call(
        functools.partial(_cumsum_kernel, scale=scale),
        out_shape=jax.ShapeDtypeStruct(x.shape, _F32),
        compiler_params=pltpu.CompilerParams(vmem_limit_bytes=_VMEM_LIMIT),
        name="logf_cumsum",
    )(x)


def _fox_kernel(q_ref, k_ref, v_ref, c_ref, o_ref, *, tq, dh, n_grp):
    i = pl.program_id(2)
    row = lax.broadcasted_iota(jnp.int32, (tq, tq), 0)
    col = lax.broadcasted_iota(jnp.int32, (tq, tq), 1)
    q0 = pl.multiple_of(i * tq, tq)
    qs = [q_ref[0, :, g * dh:(g + 1) * dh] for g in range(n_grp)]
    cqs = [jnp.sum(jnp.where(row == col, c_ref[g, :, pl.ds(q0, tq)], 0.0), axis=1, keepdims=True)
           for g in range(n_grp)]

    def block(j, carry, masked):
        k0 = pl.multiple_of(j * tq, tq)
        out = []
        for g in range(n_grp):
            m, l, acc = carry[g]
            lanes = slice(g * dh, (g + 1) * dh)
            s = lax.dot_general(qs[g], k_ref[0, pl.ds(k0, tq), lanes], _NT, preferred_element_type=_F32)
            u = s - c_ref[g, :, pl.ds(k0, tq)]
            if masked:
                u = jnp.where(col <= row, u, _NEG)
            m_new = jnp.maximum(m, jnp.max(u, axis=1, keepdims=True) + cqs[g])
            alpha = jnp.exp2(m - m_new)
            p = jnp.exp2(u - (m_new - cqs[g]))
            l = alpha * l + jnp.sum(p, axis=1, keepdims=True)
            acc = alpha * acc + jnp.dot(p.astype(_BF16), v_ref[0, pl.ds(k0, tq), lanes],
                                        preferred_element_type=_F32)
            out.append((m_new, l, acc))
        return tuple(out)

    init = tuple((jnp.full((tq, 1), _NEG, _F32), jnp.zeros((tq, 1), _F32), jnp.zeros((tq, dh), _F32))
                 for _ in range(n_grp))
    carry = lax.fori_loop(0, i, lambda j, c: block(j, c, False), init)
    carry = block(i, carry, True)
    for g in range(n_grp):
        _, l, acc = carry[g]
        o_ref[0, :, g * dh:(g + 1) * dh] = (acc / l).astype(o_ref.dtype)


def _fox_attention(q, k, v, c_rows, b, l, h, dh):
    tq = _divisor_tile(l, 512, _LANES)
    n_grp = 2 if h % 2 == 0 else 1
    qo_spec = pl.BlockSpec((1, tq, n_grp * dh), lambda bi, hi, i: (bi, i, hi))
    kv_spec = pl.BlockSpec((1, l, n_grp * dh), lambda bi, hi, i: (bi, 0, hi))
    c_spec = pl.BlockSpec((n_grp, 1, l), lambda bi, hi, i: (bi * (h // n_grp) + hi, 0, 0))
    return pl.pallas_call(
        functools.partial(_fox_kernel, tq=tq, dh=dh, n_grp=n_grp),
        grid=(b, h // n_grp, l // tq),
        in_specs=[qo_spec, kv_spec, kv_spec, c_spec],
        out_specs=qo_spec,
        out_shape=jax.ShapeDtypeStruct((b, l, h * dh), _BF16),
        compiler_params=_params("arbitrary", "arbitrary", "arbitrary"),
        name="fox_attention",
    )(q, k, v, c_rows)


def _suffix_kernel(x_ref, sfx_ref, tot_ref, *, stride):
    x = x_ref[...]
    n = x.shape[1]
    col = lax.broadcasted_iota(jnp.int32, x.shape, 1)
    inc = x
    tot = x
    s = stride
    while s < n:
        inc = inc + jnp.where(col + s < n, pltpu.roll(inc, n - s, axis=1), 0.0)
        tot = tot + pltpu.roll(tot, n - s, axis=1)
        s *= 2
    sfx_ref[...] = inc - x
    tot_ref[...] = tot


def _page_suffix(cache_logf):
    n_phys, n_fox, ps, h = cache_logf.shape
    assert ps & (ps - 1) == 0 and (ps * h) % _LANES == 0, (ps, h)
    rows = n_phys * n_fox
    tr = _divisor_tile(rows, 256, 8)
    spec = pl.BlockSpec((tr, ps * h), lambda i: (i, 0))
    sfx, tot = pl.pallas_call(
        functools.partial(_suffix_kernel, stride=h),
        grid=(rows // tr,),
        in_specs=[spec],
        out_specs=[spec, spec],
        out_shape=[jax.ShapeDtypeStruct((rows, ps * h), _F32)] * 2,
        compiler_params=_params("arbitrary"),
        name="page_suffix",
    )(cache_logf.reshape(rows, ps * h))
    return sfx, tot


def _decode_kernel(pt_ref, q_ref, cq_ref, cn_ref, kn_ref, vn_ref, mask_ref, *refs, n_heads, n_q, dh, n_grp, n_steps,
                   n_pages, n_layers, layer):
    kps, vps = refs[:n_grp], refs[n_grp:2 * n_grp]
    sfs, tts = refs[2 * n_grp:3 * n_grp], refs[3 * n_grp:4 * n_grp]
    o_ref, m_s, l_s, acc_s, carry_s = refs[4 * n_grp:]
    p = pl.program_id(1)
    q = q_ref[0]
    cq = cq_ref[0]

    def update(us, vs):
        m_old = m_s[...]
        mu = functools.reduce(jnp.maximum, [jnp.max(u, axis=1, keepdims=True) for u in us])
        m_new = jnp.maximum(m_old, mu + cq)
        shift = m_new - cq
        alpha = jnp.exp(m_old - m_new)
        l = alpha * l_s[...]
        acc = alpha * acc_s[...]
        for u, v in zip(us, vs):
            pr = jnp.exp(u - shift)
            l = l + jnp.sum(pr, axis=1, keepdims=True)
            acc = acc + jnp.dot(pr.astype(_BF16), v, preferred_element_type=_F32)
        m_s[...] = m_new
        l_s[...] = l
        acc_s[...] = acc

    @pl.when(p == 0)
    def _():
        m_s[...] = jnp.full(m_s.shape, _NEG, _F32)
        l_s[...] = jnp.zeros(l_s.shape, _F32)
        acc_s[...] = jnp.zeros(acc_s.shape, _F32)
        carry_s[...] = jnp.zeros(carry_s.shape, _F32)
        s = lax.dot_general(q, kn_ref[0], _NT, preferred_element_type=_F32)
        row = lax.broadcasted_iota(jnp.int32, s.shape, 0)
        col = lax.broadcasted_iota(jnp.int32, s.shape, 1)
        valid = (lax.rem(col, n_heads) == row // n_q) & (col // n_heads <= lax.rem(row, n_q))
        update([jnp.where(valid, s - cn_ref[0], _NEG)], [vn_ref[0]])

    @pl.when(p > 0)
    def _():
        carry = carry_s[...]
        us, vs = [], []
        for g in range(n_grp):
            page = pt_ref[pl.program_id(0), n_pages - 1 - (p - 1) * n_grp - g]
            row = pl.ds(lax.rem(page * n_layers + layer, _SUBLANES), 1)
            s = lax.dot_general(q, kps[g][...].astype(_BF16), _NT, preferred_element_type=_F32)
            us.append(s + ((sfs[g][row, :] + carry) + mask_ref[...]))
            carry = carry + tts[g][row, :]
            vs.append(vps[g][...].astype(_BF16))
        carry_s[...] = carry
        update(us, vs)

    @pl.when(p == n_steps - 1)
    def _():
        acc = acc_s[...] / l_s[...]
        for hh in range(n_heads):
            o_ref[0, :, hh * dh:(hh + 1) * dh] = acc[hh * n_q:(hh + 1) * n_q, :]


def _fox_decode(q, k_new, v_new, c_new, cache_k, cache_v, sfx, tot, page_table, layer):
    b, nq, h, dh = q.shape
    n_phys, n_fox, ps = cache_k.shape[:3]
    n_pages = page_table.shape[1]
    r = h * nq
    n_grp = next(g for g in (4, 2, 1) if n_pages % g == 0)
    n_steps = n_pages // n_grp + 1
    qs = (q * dh ** -0.5).transpose(0, 2, 1, 3).reshape(b, r, dh).astype(_BF16)
    cq = c_new.transpose(0, 2, 1).reshape(b, r, 1)
    row = lax.broadcasted_iota(jnp.int32, (r, ps * h), 0)
    col = lax.broadcasted_iota(jnp.int32, (r, ps * h), 1)
    mask = jnp.where(col % h == row // nq, 0.0, _NEG).astype(_F32)

    page_of = lambda bi, p, pt, g: pt[bi, n_pages - 1 - (jnp.maximum(p, 1) - 1) * n_grp - g]

    def page_spec(g):
        return pl.BlockSpec((None, None, ps * h, dh), lambda bi, p, pt: (page_of(bi, p, pt, g), layer, 0, 0))

    def group_spec(g):
        return pl.BlockSpec((_SUBLANES, ps * h),
                            lambda bi, p, pt: ((page_of(bi, p, pt, g) * n_fox + layer) // _SUBLANES, 0))

    per_seq = lambda shape: pl.BlockSpec((1,) + shape, lambda bi, p, pt: (bi, 0, 0))
    grid_spec = pltpu.PrefetchScalarGridSpec(
        num_scalar_prefetch=1,
        grid=(b, n_steps),
        in_specs=[per_seq((r, dh)), per_seq((r, 1)), per_seq((1, nq * h)), per_seq((nq * h, dh)),
                  per_seq((nq * h, dh)), pl.BlockSpec((r, ps * h), lambda bi, p, pt: (0, 0))]
                 + [page_spec(g) for g in range(n_grp)] * 2
                 + [group_spec(g) for g in range(n_grp)] * 2,
        out_specs=per_seq((nq, h * dh)),
        scratch_shapes=[pltpu.VMEM((r, 1), _F32), pltpu.VMEM((r, 1), _F32), pltpu.VMEM((r, dh), _F32),
                        pltpu.VMEM((1, ps * h), _F32)],
    )
    kc = cache_k.reshape(n_phys, n_fox, ps * h, dh)
    vc = cache_v.reshape(n_phys, n_fox, ps * h, dh)
    out = pl.pallas_call(
        functools.partial(_decode_kernel, n_heads=h, n_q=nq, dh=dh, n_grp=n_grp, n_steps=n_steps,
                          n_pages=n_pages, n_layers=n_fox, layer=layer),
        grid_spec=grid_spec,
        out_shape=jax.ShapeDtypeStruct((b, nq, h * dh), _F32),
        compiler_params=_params("arbitrary", "arbitrary"),
        name="fox_decode",
    )(page_table, qs, cq, c_new.reshape(b, 1, nq * h), k_new.reshape(b, nq * h, dh).astype(_BF16),
      v_new.reshape(b, nq * h, dh).astype(_BF16), mask, *([kc] * n_grp), *([vc] * n_grp),
      *([sfx] * n_grp), *([tot] * n_grp))
    return out.reshape(b * nq, h * dh).astype(_BF16)


def _first(accs, emn, en):
    return [accs[0]]


def _rwkv_layer(x, shift_prev, wkv_prev, v_first, j, i, W, Wb):
    b, l, d = x.shape
    h, n = wkv_prev.shape[1], wkv_prev.shape[2]
    xr, xw, xk, xv, xa, xg = _rwkv_mix(x, shift_prev, W['norm_mix'][i], W['rwkv_mu'][j])
    (r,) = _mm([xr], [Wb['rwkv_w_r'][j]], epilogue=_first, out_dtypes=[_F32], name="rwkv_r")
    decay = _lora(xw, W['rwkv_w1'][j], W['rwkv_w2'][j], act=jnp.tanh, extras_n=[W['rwkv_w0'][j]],
                  epilogue=lambda acc, en: jnp.exp(-jnp.exp(-_softplus(-(en[0] + acc)) - 0.5)), name="rwkv_decay")
    a = _lora(xa, W['rwkv_a1'][j], W['rwkv_a2'][j], act=lambda t: t, extras_n=[W['rwkv_a0'][j]],
              epilogue=lambda acc, en: _sigmoid(en[0] + acc), name="rwkv_a")
    g = _lora(xg, W['rwkv_g1'][j], W['rwkv_g2'][j], act=_sigmoid, epilogue=lambda acc, en: acc, name="rwkv_g")
    if j == 0:
        (v,) = _mm([xv], [Wb['rwkv_w_v'][j]], epilogue=_first, out_dtypes=[_F32], name="rwkv_v")
        v_first = v
    else:
        gate = _lora(xv, W['rwkv_v1'][j - 1], W['rwkv_v2'][j - 1], act=lambda t: t, extras_n=[W['rwkv_v0'][j - 1]],
                     epilogue=lambda acc, en: _sigmoid(en[0] + acc), name="rwkv_vgate")
        (v,) = _mm([xv], [Wb['rwkv_w_v'][j]], extras_mn=[v_first, gate], out_dtypes=[_F32], name="rwkv_v",
                   epilogue=lambda accs, emn, en: [accs[0] + (emn[0] - accs[0]) * emn[1]])
    (k,) = _mm([xk], [Wb['rwkv_w_k'][j]], epilogue=_first, out_dtypes=[_F32], name="rwkv_k")
    p = -(-(b * h) // _LANES) * _LANES
    seqs = [_to_lanes(t, b, l, h, n, p) for t in (r, decay, k, v, a)]
    params = [_param_lanes(w.reshape(-1), b, h, n, p) for w in
              (W['rwkv_k_k'][j], W['rwkv_k_a'][j], W['rwkv_r_k'][j], W['rwkv_gn_w'][j], W['rwkv_gn_b'][j])]
    s0 = jnp.pad(wkv_prev.astype(_F32).transpose(3, 2, 0, 1).reshape(n, n, b * h), ((0, 0), (0, 0), (0, p - b * h)))
    o, s_last = _rwkv_scan(seqs, params, s0, n)
    og = _from_lanes(o, g, b, l, h, n)
    s_last = s_last[:, :, :b * h].reshape(n, n, b, h).transpose(2, 3, 1, 0)
    x2 = x.reshape(b * l, d)
    (x2,) = _mm([og], [Wb['rwkv_w_o'][j]], extras_mn=[x2], out_dtypes=[_F32], name="rwkv_out",
                epilogue=lambda accs, emn, en: [emn[0] + accs[0]])
    shift_out = _rmsnorm(x[:, -1, :], W['norm_mix'][i], _F32)
    return x2.reshape(b, l, d), shift_out, s_last, v_first


def _fox_layer(x, j, i, W, Wb, paged, kv_bufs):
    b, l, d = x.shape
    h = W['fox_b_f'].shape[1]
    dh = d // h
    x2 = x.reshape(b * l, d)
    hn = _rmsnorm(x2, W['norm_mix'][i], _BF16)
    w_in = Wb['fox_in'][j]
    w_q, w_k, w_v = ((c * d, d) for c in range(3))
    w_f = (3 * d, -(-h // _LANES) * _LANES)
    if paged is None:
        q_scale = dh ** -0.5 * _LOG2E
        (q,) = _mm([hn], [w_in], window=w_q, epilogue=lambda accs, emn, en: [accs[0] * q_scale],
                   out_dtypes=[_BF16], name="fox_q")
    else:
        (q,) = _mm([hn], [w_in], window=w_q, epilogue=_first, out_dtypes=[_F32], name="fox_q")
    twice = lambda accs, emn, en: [accs[0], accs[0]]
    stack_k, stack_v = (None, None) if kv_bufs is None else ((kv_bufs[0], j, l), (kv_bufs[1], j, l))
    k, kb = _mm([hn], [w_in], window=w_k, epilogue=twice, out_dtypes=[_F32, _BF16], name="fox_k", stack=stack_k)
    v, vb = _mm([hn], [w_in], window=w_v, epilogue=twice, out_dtypes=[_F32, _BF16], name="fox_v", stack=stack_v)
    bias_f = jnp.pad(W['fox_b_f'][j], (0, w_f[1] - h))
    (logf_pad,) = _mm([hn], [w_in], window=w_f, extras_n=[bias_f], out_dtypes=[_F32], name="fox_logf",
                      epilogue=lambda accs, emn, en: [-_softplus(-(accs[0] + en[0]))])
    logf = logf_pad[:, :h].reshape(b, l, h)
    c_rows = _cumsum_rows(logf.transpose(0, 2, 1).reshape(b * h, l), _LOG2E if paged is None else 1.0)
    if paged is None:
        o = _fox_attention(q.reshape(b, l, d), kb.reshape(b, l, d), vb.reshape(b, l, d),
                           c_rows.reshape(b * h, 1, l), b, l, h, dh)
        o = o.reshape(b * l, d)
    else:
        cache_k, cache_v, sfx, tot, page_table = paged
        c_new = c_rows.reshape(b, h, l).transpose(0, 2, 1)
        o = _fox_decode(q.reshape(b, l, h, dh), k.reshape(b, l, h, dh), v.reshape(b, l, h, dh), c_new,
                        cache_k, cache_v, sfx, tot, page_table, j)
    (x2,) = _mm([o], [Wb['fox_w_o'][j]], extras_mn=[x2], out_dtypes=[_F32], name="fox_out",
                epilogue=lambda accs, emn, en: [emn[0] + accs[0]])
    return x2.reshape(b, l, d), k, v, logf


def _ffn_ple(x, p_i, i, W, Wb):
    b, l, d = x.shape
    x2 = x.reshape(b * l, d)
    hn = _rmsnorm(x2, W['norm_ffn'][i], _BF16)
    (up,) = _mm([hn], [Wb['w_up'][i]], out_dtypes=[_BF16], name="ffn_up",
                epilogue=lambda accs, emn, en: [jnp.square(jnp.maximum(accs[0], 0.0))])
    (x2,) = _mm([up], [Wb['w_down'][i]], extras_mn=[x2], out_dtypes=[_F32], name="ffn_down",
                epilogue=lambda accs, emn, en: [emn[0] + accs[0]])
    hp = _rmsnorm(x2, W['norm_ple'][i], _BF16)
    pb = p_i.reshape(b * l, -1).astype(_BF16)
    (x2,) = _mm([hp, pb], [Wb['w_ple_gate'][i], Wb['w_ple'][i]], extras_mn=[x2], out_dtypes=[_F32], name="ple",
                epilogue=lambda accs, emn, en: [emn[0] + _sigmoid(accs[0]) * accs[1]])
    return x2.reshape(b, l, d)


def _trunk(x, p, shift_in, wkv_in, paged, W, Wb):
    depth = W['norm_mix'].shape[0]
    b, l, d = x.shape
    n_fox = depth // 2
    h = W['fox_b_f'].shape[1]
    kv_bufs = None
    if paged is None and l % _LANES == 0:
        rows = b * n_fox * l * (d // _LANES)
        kv_bufs = [jnp.zeros((rows, _LANES), _F32), jnp.zeros((rows, _LANES), _F32)]
    v_first = None
    shifts, wkvs, ks, vs, lfs = [], [], [], [], []
    for i in range(depth):
        j = i // 2
        if i % 2 == 0:
            x, sh, s_last, v_first = _rwkv_layer(x, shift_in[j], wkv_in[j], v_first, j, i, W, Wb)
            shifts.append(sh)
            wkvs.append(s_last)
        else:
            x, k, v, lf = _fox_layer(x, j, i, W, Wb, paged, kv_bufs)
            if kv_bufs is not None:
                kv_bufs = [k, v]
            else:
                ks.append(k.reshape(b, l, h, d // h))
                vs.append(v.reshape(b, l, h, d // h))
            lfs.append(lf)
        x = _ffn_ple(x, p[i], i, W, Wb)
    y = _rmsnorm(x.reshape(b * l, d), W['norm_out'], _F32).reshape(b, l, d)
    if kv_bufs is None:
        k_all, v_all = jnp.stack(ks, 1), jnp.stack(vs, 1)
    else:
        k_all, v_all = (t.reshape(b, n_fox, l, h, d // h) for t in kv_bufs)
    return y, jnp.stack(shifts, 0), jnp.stack(wkvs, 0), k_all, v_all, jnp.stack(lfs, 1)


def kernel(x_prompt, x_sample, cache_k, cache_v, cache_logf, state_wkv, state_shift, page_table, p_prompt, p_sample, norm_mix, norm_ffn, norm_ple, norm_out, rwkv_mu, rwkv_w_r, rwkv_w_k, rwkv_w_v, rwkv_w_o, rwkv_w0, rwkv_w1, rwkv_w2, rwkv_a0, rwkv_a1, rwkv_a2, rwkv_v0, rwkv_v1, rwkv_v2, rwkv_g1, rwkv_g2, rwkv_k_k, rwkv_k_a, rwkv_r_k, rwkv_gn_w, rwkv_gn_b, fox_w_in, fox_b_f, fox_w_o, w_up, w_down, w_ple, w_ple_gate):
    W = {
        'norm_mix': norm_mix, 'norm_ffn': norm_ffn, 'norm_ple': norm_ple, 'norm_out': norm_out,
        'rwkv_mu': rwkv_mu, 'rwkv_w0': rwkv_w0, 'rwkv_w1': rwkv_w1, 'rwkv_w2': rwkv_w2,
        'rwkv_a0': rwkv_a0, 'rwkv_a1': rwkv_a1, 'rwkv_a2': rwkv_a2,
        'rwkv_v0': rwkv_v0, 'rwkv_v1': rwkv_v1, 'rwkv_v2': rwkv_v2,
        'rwkv_g1': rwkv_g1, 'rwkv_g2': rwkv_g2, 'rwkv_k_k': rwkv_k_k, 'rwkv_k_a': rwkv_k_a,
        'rwkv_r_k': rwkv_r_k, 'rwkv_gn_w': rwkv_gn_w, 'rwkv_gn_b': rwkv_gn_b, 'fox_b_f': fox_b_f,
    }
    d = x_prompt.shape[-1]
    bf = lambda w: w.astype(_BF16)
    Wb = {
        'rwkv_w_r': bf(rwkv_w_r), 'rwkv_w_k': bf(rwkv_w_k), 'rwkv_w_v': bf(rwkv_w_v), 'rwkv_w_o': bf(rwkv_w_o),
        'fox_in': bf(fox_w_in),
        'fox_w_o': bf(fox_w_o), 'w_up': bf(w_up), 'w_down': bf(w_down),
        'w_ple': bf(w_ple), 'w_ple_gate': bf(w_ple_gate),
    }
    n_rwkv, _, n_heads, n_dim, _ = state_wkv.shape
    b = x_prompt.shape[0]
    shift0 = jnp.zeros((n_rwkv, b, d), x_prompt.dtype)
    wkv0 = jnp.zeros((n_rwkv, b, n_heads, n_dim, n_dim), _F32)
    y_p, shift_p, wkv_p, k_p, v_p, lf_p = _trunk(x_prompt, p_prompt, shift0, wkv0, None, W, Wb)
    sfx, tot = _page_suffix(cache_logf)
    y_s, shift_s, wkv_s, k_s, v_s, lf_s = _trunk(
        x_sample, p_sample, state_shift, state_wkv, (cache_k, cache_v, sfx, tot, page_table), W, Wb)
    return (y_p, y_s, k_p, v_p, lf_p, k_s, v_s, lf_s, wkv_p, shift_p, wkv_s, shift_s)
```

```python
import functools

import jax
import jax.numpy as jnp
from jax import lax
from jax.experimental import pallas as pl
from jax.experimental.pallas import tpu as pltpu

_RMS_EPS = 1e-6
_GN_EPS = 64e-5
_KK_EPS = 1e-24
_NEG = -1e30
_LOG2E = 1.4426950408889634
_LANES = 128
_SUBLANES = 8
_SCAN_STEPS = 64
_ROW_PAD = 8
_VMEM_LIMIT = 56 * 1024 * 1024
_VMEM_BUDGET = 40 * 1024 * 1024
_F32 = jnp.float32
_BF16 = jnp.bfloat16
_NT = (((1,), (1,)), ((), ()))


def _params(*sem):
    return pltpu.CompilerParams(dimension_semantics=sem, vmem_limit_bytes=_VMEM_LIMIT)


def _divisor_tile(n, target, align):
    t = min(n, target)
    t -= t % align
    while t >= align:
        if n % t == 0:
            return t
        t -= align
    return n


def _sigmoid(x):
    return 1.0 / (1.0 + jnp.exp(-x))


def _softplus(x):
    return jnp.maximum(x, 0.0) + jnp.log(1.0 + jnp.exp(-jnp.abs(x)))


def _rms(x, g):
    return x * lax.rsqrt(jnp.mean(x * x, axis=-1, keepdims=True) + _RMS_EPS) * g


def _rmsnorm_kernel(x_ref, g_ref, o_ref):
    o_ref[...] = _rms(x_ref[...], g_ref[...]).astype(o_ref.dtype)


def _rmsnorm(x, g, out_dtype):
    t, d = x.shape
    tr = _divisor_tile(t, 512, 16)
    return pl.pallas_call(
        _rmsnorm_kernel,
        grid=(t // tr,),
        in_specs=[pl.BlockSpec((tr, d), lambda i: (i, 0)), pl.BlockSpec((1, d), lambda i: (0, 0))],
        out_specs=pl.BlockSpec((tr, d), lambda i: (i, 0)),
        out_shape=jax.ShapeDtypeStruct((t, d), out_dtype),
        compiler_params=_params("arbitrary"),
        name="rmsnorm",
    )(x, g.reshape(1, d))


def _mix_kernel(x_ref, xp_ref, sh_ref, g_ref, mu_ref, *o_refs):
    g = g_ref[...]
    xn = _rms(x_ref[0], g)
    prev = _rms(xp_ref[0, 7:8, :], g)
    prev = jnp.where(pl.program_id(1) == 0, sh_ref[0], prev)
    row = lax.broadcasted_iota(jnp.int32, xn.shape, 0)
    xprev = jnp.where(row == 0, prev, pltpu.roll(xn, 1, axis=0))
    xx = xprev - xn
    for c, o_ref in enumerate(o_refs):
        o_ref[0] = (xn + xx * mu_ref[c:c + 1, :]).astype(o_ref.dtype)


def _rwkv_mix(x, shift_prev, g, mu):
    b, l, d = x.shape
    tr = _divisor_tile(l, 256, 8)
    nb8 = tr // 8
    outs = pl.pallas_call(
        _mix_kernel,
        grid=(b, l // tr),
        in_specs=[
            pl.BlockSpec((1, tr, d), lambda bi, i: (bi, i, 0)),
            pl.BlockSpec((1, 8, d), lambda bi, i: (bi, jnp.maximum(i * nb8 - 1, 0), 0)),
            pl.BlockSpec((1, 1, d), lambda bi, i: (bi, 0, 0)),
            pl.BlockSpec((1, d), lambda bi, i: (0, 0)),
            pl.BlockSpec((6, d), lambda bi, i: (0, 0)),
        ],
        out_specs=[pl.BlockSpec((1, tr, d), lambda bi, i: (bi, i, 0))] * 6,
        out_shape=[jax.ShapeDtypeStruct((b, l, d), _BF16)] * 6,
        compiler_params=_params("arbitrary", "arbitrary"),
        name="rwkv_mix",
    )(x, x, shift_prev.reshape(b, 1, d), g.reshape(1, d), mu)
    return [o.reshape(b * l, d) for o in outs]


def _mm_kernel(*refs, n_w, n_emn, n_en, n_skip, split, epilogue):
    xs = [refs[i][...] for i in range(n_w)]
    ws = [refs[n_w + i][...] for i in range(n_w)]
    pos = 2 * n_w
    emn = [refs[pos + i][...] for i in range(n_emn)]
    pos += n_emn
    en = [refs[pos + i][...] for i in range(n_en)]
    pos += n_en
    accs = [jnp.dot(x, w, preferred_element_type=_F32) for x, w in zip(xs, ws)]
    outs = epilogue(accs, emn, en)
    out_refs = refs[pos + n_skip:]
    if split:
        o0 = outs[0].astype(out_refs[0].dtype)
        for g in range(split):
            out_refs[0][pl.ds(g, o0.shape[0], stride=split), :] = o0[:, g * _LANES:(g + 1) * _LANES]
        out_refs, outs = out_refs[1:], outs[1:]
    for o_ref, o in zip(out_refs, outs):
        o_ref[...] = o.astype(o_ref.dtype)


def _mm_tiles(m, n, x_row_bytes, k_total, n_emn, out_bytes, m_unit, n_unit):
    best = None
    for tm in (1024, 512, 256, 128, 64, 32, 16, 8):
        if tm > m or m_unit % tm:
            continue
        for tn in (2048, 1024, 512, 256, 128):
            if tn > n or n % tn or tn % n_unit:
                continue
            need = 2 * (tm * x_row_bytes + k_total * tn * 2 + tm * tn * (4 * n_emn + out_bytes)) + tm * tn * 4
            if need > _VMEM_BUDGET:
                continue
            score = (tm * tn / (tm + tn), tm)
            if best is None or score > best[0]:
                best = (score, tm, tn)
    assert best is not None, (m, n)
    return best[1], best[2]


def _mm(xs, ws, *, epilogue, out_dtypes, extras_mn=(), extras_n=(), name="mm", stack=None, window=None):
    m = xs[0].shape[0]
    col0, n = (0, ws[0].shape[1]) if window is None else window
    x_row_bytes = sum(a.shape[1] * a.dtype.itemsize for a in xs)
    k_total = sum(w.shape[0] for w in ws)
    out_bytes = sum(jnp.dtype(dt).itemsize for dt in out_dtypes)
    tm, tn = _mm_tiles(m, n, x_row_bytes, k_total, len(extras_mn), out_bytes, m if stack is None else stack[2],
                       _LANES if stack is None else n)
    assert col0 % tn == 0, (col0, tn)
    in_specs = [pl.BlockSpec((tm, a.shape[1]), lambda i, j: (i, 0)) for a in xs]
    in_specs += [pl.BlockSpec((w.shape[0], tn), lambda i, j: (0, j + col0 // tn)) for w in ws]
    in_specs += [pl.BlockSpec((tm, tn), lambda i, j: (i, j)) for _ in extras_mn]
    in_specs += [pl.BlockSpec((1, tn), lambda i, j: (0, j)) for _ in extras_n]
    split = 0 if stack is None else n // _LANES
    kern = functools.partial(_mm_kernel, n_w=len(ws), n_emn=len(extras_mn), n_en=len(extras_n),
                             n_skip=0 if stack is None else 1, split=split, epilogue=epilogue)
    out_specs = [pl.BlockSpec((tm, tn), lambda i, j: (i, j)) for _ in out_dtypes]
    out_shape = [jax.ShapeDtypeStruct((m, n), dt) for dt in out_dtypes]
    operands = [*xs, *ws, *extras_mn, *[e.reshape(1, n) for e in extras_n]]
    aliases = {}
    if stack is not None:
        buf, slot, seq = stack
        per = seq // tm
        n_slots = buf.shape[0] // (m * split)
        out_specs[0] = pl.BlockSpec((tm * split, _LANES), lambda i, j: ((i // per * n_slots + slot) * per + i % per, 0))
        out_shape[0] = jax.ShapeDtypeStruct(buf.shape, buf.dtype)
        in_specs.append(pl.BlockSpec(memory_space=pl.ANY))
        aliases = {len(operands): 0}
        operands.append(buf)
    return pl.pallas_call(
        kern,
        grid=(m // tm, n // tn),
        in_specs=in_specs,
        out_specs=out_specs,
        out_shape=out_shape,
        input_output_aliases=aliases,
        compiler_params=_params("arbitrary", "arbitrary"),
        name=name,
    )(*operands)


def _lora_kernel(x_ref, w1_ref, w2_ref, *refs, act, n_en, epilogue):
    h = jnp.dot(x_ref[...], w1_ref[...], preferred_element_type=_F32)
    acc = jnp.dot(act(h).astype(_BF16), w2_ref[...], preferred_element_type=_F32)
    en = [refs[i][...] for i in range(n_en)]
    refs[n_en][...] = epilogue(acc, en).astype(refs[n_en].dtype)


def _lora(x, w1, w2, *, act, epilogue, extras_n=(), name="lora"):
    m, k = x.shape
    r = w1.shape[1]
    n = w2.shape[1]
    rp = -(-r // _LANES) * _LANES
    w1 = jnp.pad(w1, ((0, 0), (0, rp - r))).astype(_BF16)
    w2 = jnp.pad(w2, ((0, rp - r), (0, 0))).astype(_BF16)
    tm = _divisor_tile(m, 512, 16)
    kern = functools.partial(_lora_kernel, act=act, n_en=len(extras_n), epilogue=epilogue)
    return pl.pallas_call(
        kern,
        grid=(m // tm,),
        in_specs=[pl.BlockSpec((tm, k), lambda i: (i, 0)),
                  pl.BlockSpec((k, rp), lambda i: (0, 0)),
                  pl.BlockSpec((rp, n), lambda i: (0, 0))]
                 + [pl.BlockSpec((1, n), lambda i: (0, 0)) for _ in extras_n],
        out_specs=pl.BlockSpec((tm, n), lambda i: (i, 0)),
        out_shape=jax.ShapeDtypeStruct((m, n), _F32),
        compiler_params=_params("arbitrary"),
        name=name,
    )(x, w1, w2, *[e.reshape(1, n) for e in extras_n])


def _scan_kernel(r_ref, w_ref, k_ref, v_ref, a_ref, kkp_ref, kap_ref, rk_ref, gw_ref, gb_ref, s0_ref,
                 o_ref, s_ref, kk_s, kka_s, km_s, rr_s, gam_s, *, tb, n):
    @pl.when(pl.program_id(1) == 0)
    def _():
        s_ref[...] = s0_ref[...]

    gam_s[...] = jnp.ones(gam_s.shape, _F32)

    def step(t, carry):
        k_t = k_ref[t, 0:n, :]
        a_t = a_ref[t, 0:n, :]
        r_t = r_ref[t, 0:n, :]
        kk = k_t * kkp_ref[...]
        kk = kk * lax.rsqrt(jnp.maximum(jnp.sum(kk * kk, axis=0, keepdims=True), _KK_EPS))
        km = k_t * (1.0 + (a_t - 1.0) * kap_ref[...])
        gam_prev = gam_s[...]
        gam = gam_prev * w_ref[t, 0:n, :]
        inv = 1.0 / gam
        gam_s[...] = gam
        kk_s[...] = kk * gam_prev
        kka_s[...] = kk * a_t * inv
        km_s[...] = km * inv
        rr_s[...] = r_t * gam
        v_t = v_ref[t, 0:n, :]
        sk = [jnp.zeros_like(v_t), jnp.zeros_like(v_t)]
        for j in range(n):
            sk[j % 2] = sk[j % 2] + s_ref[j] * kk_s[j:j + 1, :]
        sk = sk[0] + sk[1]
        o = [jnp.zeros_like(v_t), jnp.zeros_like(v_t)]
        for j in range(n):
            s_new = s_ref[j] + (v_t * km_s[j:j + 1, :] - sk * kka_s[j:j + 1, :])
            s_ref[j] = s_new
            o[j % 2] = o[j % 2] + s_new * rr_s[j:j + 1, :]
        o = o[0] + o[1]
        mean = jnp.mean(o, axis=0, keepdims=True)
        var = jnp.mean(jnp.square(o - mean), axis=0, keepdims=True)
        o = (o - mean) * lax.rsqrt(var + _GN_EPS) * gw_ref[...] + gb_ref[...]
        bonus = jnp.sum(r_t * km * rk_ref[...], axis=0, keepdims=True) * v_t
        o_ref[t, 0:n, :] = o + bonus
        o_ref[t, n:, :] = jnp.zeros((o_ref.shape[1] - n, _LANES), _F32)
        return carry

    lax.fori_loop(0, tb, step, 0)
    for j in range(n):
        s_ref[j] = s_ref[j] * gam_s[j:j + 1, :]


def _rwkv_scan(seqs, params, s0, n):
    l, rows, p = seqs[0].shape
    tb = _divisor_tile(l, _SCAN_STEPS, 1)
    seq_spec = pl.BlockSpec((tb, rows, _LANES), lambda c, i: (i, 0, c))
    par_spec = pl.BlockSpec((n, _LANES), lambda c, i: (0, c))
    st_spec = pl.BlockSpec((n, n, _LANES), lambda c, i: (0, 0, c))
    return pl.pallas_call(
        functools.partial(_scan_kernel, tb=tb, n=n),
        grid=(p // _LANES, l // tb),
        in_specs=[seq_spec] * 5 + [par_spec] * 5 + [st_spec],
        out_specs=[seq_spec, st_spec],
        out_shape=[jax.ShapeDtypeStruct((l, rows, p), _F32), jax.ShapeDtypeStruct((n, n, p), _F32)],
        scratch_shapes=[pltpu.VMEM((n, _LANES), _F32)] * 5,
        compiler_params=_params("arbitrary", "arbitrary"),
        name="rwkv_scan",
    )(*seqs, *params, s0)


def _lanes_in_pallas(b, l, h, n):
    return l % _LANES == 0 and _LANES % h == 0 and b % (_LANES // h) == 0 and _LANES % n == 0


def _to_lanes_kernel(x_ref, o_ref, y_s, *, n):
    g, steps, d = x_ref.shape
    tiles = d // _LANES
    per = _LANES // n
    pitch = o_ref.shape[0] // steps
    for b in range(g):
        for c in range(tiles):
            t = x_ref[b, :, c * _LANES:(c + 1) * _LANES].T
            for hh in range(per):
                r0 = ((b * tiles + c) * per + hh) * pitch
                y_s[r0:r0 + n, :] = t[hh * n:(hh + 1) * n, :]
    for j in range(n):
        o_ref[pl.ds(j, steps, stride=pitch), :] = y_s[pl.ds(j, _LANES, stride=pitch), :].T
    for j in range(n, pitch):
        o_ref[pl.ds(j, steps, stride=pitch), :] = jnp.zeros((steps, _LANES), _F32)


def _from_lanes_kernel(o_ref, g_ref, out_ref, y_s, *, n):
    g, steps, d = g_ref.shape
    tiles = d // _LANES
    per = _LANES // n
    pitch = o_ref.shape[0] // steps
    for j in range(n):
        y_s[pl.ds(j, _LANES, stride=pitch), :] = o_ref[pl.ds(j, steps, stride=pitch), :].T
    for b in range(g):
        for c in range(tiles):
            t = jnp.concatenate([y_s[((b * tiles + c) * per + hh) * pitch:((b * tiles + c) * per + hh) * pitch + n, :]
                                 for hh in range(per)], axis=0)
            cols = slice(c * _LANES, (c + 1) * _LANES)
            out_ref[b, :, cols] = (t.T * g_ref[b, :, cols]).astype(out_ref.dtype)


def _to_lanes(x, b, l, h, n, p):
    d = h * n
    rows = n + _ROW_PAD
    if not _lanes_in_pallas(b, l, h, n):
        y = x.reshape(b, l, h, n).transpose(1, 3, 0, 2).reshape(l, n, b * h)
        return jnp.pad(y, ((0, 0), (0, _ROW_PAD), (0, p - b * h)))
    g = _LANES // h
    out = pl.pallas_call(
        functools.partial(_to_lanes_kernel, n=n),
        grid=(p // _LANES, l // _LANES),
        in_specs=[pl.BlockSpec((g, _LANES, d), lambda c, i: (c, i, 0))],
        out_specs=pl.BlockSpec((_LANES * rows, _LANES), lambda c, i: (i, c)),
        out_shape=jax.ShapeDtypeStruct((l * rows, p), _F32),
        scratch_shapes=[pltpu.VMEM((_LANES * rows, _LANES), _F32)],
        compiler_params=_params("arbitrary", "arbitrary"),
        name="to_lanes",
    )(x.reshape(b, l, d))
    return out.reshape(l, rows, p)


def _from_lanes(o, gate, b, l, h, n):
    d = h * n
    rows, p = o.shape[1:]
    if not _lanes_in_pallas(b, l, h, n):
        y = o[:, :n, :b * h].reshape(l, n, b, h).transpose(2, 0, 3, 1).reshape(b * l, d)
        return (y * gate).astype(_BF16)
    g = _LANES // h
    out = pl.pallas_call(
        functools.partial(_from_lanes_kernel, n=n),
        grid=(p // _LANES, l // _LANES),
        in_specs=[pl.BlockSpec((_LANES * rows, _LANES), lambda c, i: (i, c)),
                  pl.BlockSpec((g, _LANES, d), lambda c, i: (c, i, 0))],
        out_specs=pl.BlockSpec((g, _LANES, d), lambda c, i: (c, i, 0)),
        out_shape=jax.ShapeDtypeStruct((b, l, d), _BF16),
        scratch_shapes=[pltpu.VMEM((_LANES * rows, _LANES), _F32)],
        compiler_params=_params("arbitrary", "arbitrary"),
        name="from_lanes",
    )(o.reshape(l * rows, p), gate.reshape(b, l, d))
    return out.reshape(b * l, d)


def _param_lanes(w, b, h, n, p):
    y = jnp.broadcast_to(w.reshape(h, n).T[:, None, :], (n, b, h)).reshape(n, b * h)
    return jnp.pad(y, ((0, 0), (0, p - b * h)))


def _cumsum_kernel(x_ref, o_ref, *, scale):
    x = x_ref[...] * scale
    length = x.shape[1]
    col = lax.broadcasted_iota(jnp.int32, x.shape, 1)
    if length % _LANES == 0:
        s = 1
        while s < length:
            x = x + jnp.where(col >= s, pltpu.roll(x, s, axis=1), 0.0)
            s *= 2
        o_ref[...] = x
    else:
        acc = jnp.zeros_like(x)
        for s in range(length):
            acc = acc + jnp.where(col >= s, x[:, s:s + 1], 0.0)
        o_ref[...] = acc


def _cumsum_rows(x, scale):
    return pl.pallas_call(
        functools.partial(_cumsum_kernel, scale=scale),
        out_shape=jax.ShapeDtypeStruct(x.shape, _F32),
        compiler_params=pltpu.CompilerParams(vmem_limit_bytes=_VMEM_LIMIT),
        name="logf_cumsum",
    )(x)


def _fox_kernel(q_ref, k_ref, v_ref, c_ref, o_ref, *, tq, dh, n_grp):
    i = pl.program_id(2)
    row = lax.broadcasted_iota(jnp.int32, (tq, tq), 0)
    col = lax.broadcasted_iota(jnp.int32, (tq, tq), 1)
    q0 = pl.multiple_of(i * tq, tq)
    qs = [q_ref[0, :, g * dh:(g + 1) * dh] for g in range(n_grp)]
    cqs = [jnp.sum(jnp.where(row == col, c_ref[g, :, pl.ds(q0, tq)], 0.0), axis=1, keepdims=True)
           for g in range(n_grp)]

    def block(j, carry, masked):
        k0 = pl.multiple_of(j * tq, tq)
        out = []
        for g in range(n_grp):
            m, l, acc = carry[g]
            lanes = slice(g * dh, (g + 1) * dh)
            s = lax.dot_general(qs[g], k_ref[0, pl.ds(k0, tq), lanes], _NT, preferred_element_type=_F32)
            u = s - c_ref[g, :, pl.ds(k0, tq)]
            if masked:
                u = jnp.where(col <= row, u, _NEG)
            m_new = jnp.maximum(m, jnp.max(u, axis=1, keepdims=True) + cqs[g])
            alpha = jnp.exp2(m - m_new)
            p = jnp.exp2(u - (m_new - cqs[g]))
            l = alpha * l + jnp.sum(p, axis=1, keepdims=True)
            acc = alpha * acc + jnp.dot(p.astype(_BF16), v_ref[0, pl.ds(k0, tq), lanes],
                                        preferred_element_type=_F32)
            out.append((m_new, l, acc))
        return tuple(out)

    init = tuple((jnp.full((tq, 1), _NEG, _F32), jnp.zeros((tq, 1), _F32), jnp.zeros((tq, dh), _F32))
                 for _ in range(n_grp))
    carry = lax.fori_loop(0, i, lambda j, c: block(j, c, False), init)
    carry = block(i, carry, True)
    for g in range(n_grp):
        _, l, acc = carry[g]
        o_ref[0, :, g * dh:(g + 1) * dh] = (acc / l).astype(o_ref.dtype)


def _fox_attention(q, k, v, c_rows, b, l, h, dh):
    tq = _divisor_tile(l, 512, _LANES)
    n_grp = 2 if h % 2 == 0 else 1
    qo_spec = pl.BlockSpec((1, tq, n_grp * dh), lambda bi, hi, i: (bi, i, hi))
    kv_spec = pl.BlockSpec((1, l, n_grp * dh), lambda bi, hi, i: (bi, 0, hi))
    c_spec = pl.BlockSpec((n_grp, 1, l), lambda bi, hi, i: (bi * (h // n_grp) + hi, 0, 0))
    return pl.pallas_call(
        functools.partial(_fox_kernel, tq=tq, dh=dh, n_grp=n_grp),
        grid=(b, h // n_grp, l // tq),
        in_specs=[qo_spec, kv_spec, kv_spec, c_spec],
        out_specs=qo_spec,
        out_shape=jax.ShapeDtypeStruct((b, l, h * dh), _BF16),
        compiler_params=_params("arbitrary", "arbitrary", "arbitrary"),
        name="fox_attention",
    )(q, k, v, c_rows)


def _suffix_kernel(x_ref, sfx_ref, tot_ref, *, stride):
    x = x_ref[...]
    n = x.shape[1]
    col = lax.broadcasted_iota(jnp.int32, x.shape, 1)
    inc = x
    tot = x
    s = stride
    while s < n:
        inc = inc + jnp.where(col + s < n, pltpu.roll(inc, n - s, axis=1), 0.0)
        tot = tot + pltpu.roll(tot, n - s, axis=1)
        s *= 2
    sfx_ref[...] = inc - x
    tot_ref[...] = tot


def _page_suffix(cache_logf):
    n_phys, n_fox, ps, h = cache_logf.shape
    assert ps & (ps - 1) == 0 and (ps * h) % _LANES == 0, (ps, h)
    rows = n_phys * n_fox
    tr = _divisor_tile(rows, 256, 8)
    spec = pl.BlockSpec((tr, ps * h), lambda i: (i, 0))
    sfx, tot = pl.pallas_call(
        functools.partial(_suffix_kernel, stride=h),
        grid=(rows // tr,),
        in_specs=[spec],
        out_specs=[spec, spec],
        out_shape=[jax.ShapeDtypeStruct((rows, ps * h), _F32)] * 2,
        compiler_params=_params("arbitrary"),
        name="page_suffix",
    )(cache_logf.reshape(rows, ps * h))
    return sfx, tot


def _decode_kernel(pt_ref, q_ref, cq_ref, cn_ref, kn_ref, vn_ref, mask_ref, *refs, n_heads, n_q, dh, n_grp, n_steps,
                   n_pages, n_layers, layer):
    kps, vps = refs[:n_grp], refs[n_grp:2 * n_grp]
    sfs, tts = refs[2 * n_grp:3 * n_grp], refs[3 * n_grp:4 * n_grp]
    o_ref, m_s, l_s, acc_s, carry_s = refs[4 * n_grp:]
    p = pl.program_id(1)
    q = q_ref[0]
    cq = cq_ref[0]

    def update(us, vs):
        m_old = m_s[...]
        mu = functools.reduce(jnp.maximum, [jnp.max(u, axis=1, keepdims=True) for u in us])
        m_new = jnp.maximum(m_old, mu + cq)
        shift = m_new - cq
        alpha = jnp.exp(m_old - m_new)
        l = alpha * l_s[...]
        acc = alpha * acc_s[...]
        for u, v in zip(us, vs):
            pr = jnp.exp(u - shift)
            l = l + jnp.sum(pr, axis=1, keepdims=True)
            acc = acc + jnp.dot(pr.astype(_BF16), v, preferred_element_type=_F32)
        m_s[...] = m_new
        l_s[...] = l
        acc_s[...] = acc

    @pl.when(p == 0)
    def _():
        m_s[...] = jnp.full(m_s.shape, _NEG, _F32)
        l_s[...] = jnp.zeros(l_s.shape, _F32)
        acc_s[...] = jnp.zeros(acc_s.shape, _F32)
        carry_s[...] = jnp.zeros(carry_s.shape, _F32)
        s = lax.dot_general(q, kn_ref[0], _NT, preferred_element_type=_F32)
        row = lax.broadcasted_iota(jnp.int32, s.shape, 0)
        col = lax.broadcasted_iota(jnp.int32, s.shape, 1)
        valid = (lax.rem(col, n_heads) == row // n_q) & (col // n_heads <= lax.rem(row, n_q))
        update([jnp.where(valid, s - cn_ref[0], _NEG)], [vn_ref[0]])

    @pl.when(p > 0)
    def _():
        carry = carry_s[...]
        us, vs = [], []
        for g in range(n_grp):
            page = pt_ref[pl.program_id(0), n_pages - 1 - (p - 1) * n_grp - g]
            row = pl.ds(lax.rem(page * n_layers + layer, _SUBLANES), 1)
            s = lax.dot_general(q, kps[g][...].astype(_BF16), _NT, preferred_element_type=_F32)
            us.append(s + ((sfs[g][row, :] + carry) + mask_ref[...]))
            carry = carry + tts[g][row, :]
            vs.append(vps[g][...].astype(_BF16))
        carry_s[...] = carry
        update(us, vs)

    @pl.when(p == n_steps - 1)
    def _():
        acc = acc_s[...] / l_s[...]
        for hh in range(n_heads):
            o_ref[0, :, hh * dh:(hh + 1) * dh] = acc[hh * n_q:(hh + 1) * n_q, :]


def _fox_decode(q, k_new, v_new, c_new, cache_k, cache_v, sfx, tot, page_table, layer):
    b, nq, h, dh = q.shape
    n_phys, n_fox, ps = cache_k.shape[:3]
    n_pages = page_table.shape[1]
    r = h * nq
    n_grp = next(g for g in (8, 4, 2, 1) if n_pages % g == 0)
    n_steps = n_pages // n_grp + 1
    qs = (q * dh ** -0.5).transpose(0, 2, 1, 3).reshape(b, r, dh).astype(_BF16)
    cq = c_new.transpose(0, 2, 1).reshape(b, r, 1)
    row = lax.broadcasted_iota(jnp.int32, (r, ps * h), 0)
    col = lax.broadcasted_iota(jnp.int32, (r, ps * h), 1)
    mask = jnp.where(col % h == row // nq, 0.0, _NEG).astype(_F32)

    page_of = lambda bi, p, pt, g: pt[bi, n_pages - 1 - (jnp.maximum(p, 1) - 1) * n_grp - g]

    def page_spec(g):
        return pl.BlockSpec((None, None, ps * h, dh), lambda bi, p, pt: (page_of(bi, p, pt, g), layer, 0, 0))

    def group_spec(g):
        return pl.BlockSpec((_SUBLANES, ps * h),
                            lambda bi, p, pt: ((page_of(bi, p, pt, g) * n_fox + layer) // _SUBLANES, 0))

    per_seq = lambda shape: pl.BlockSpec((1,) + shape, lambda bi, p, pt: (bi, 0, 0))
    grid_spec = pltpu.PrefetchScalarGridSpec(
        num_scalar_prefetch=1,
        grid=(b, n_steps),
        in_specs=[per_seq((r, dh)), per_seq((r, 1)), per_seq((1, nq * h)), per_seq((nq * h, dh)),
                  per_seq((nq * h, dh)), pl.BlockSpec((r, ps * h), lambda bi, p, pt: (0, 0))]
                 + [page_spec(g) for g in range(n_grp)] * 2
                 + [group_spec(g) for g in range(n_grp)] * 2,
        out_specs=per_seq((nq, h * dh)),
        scratch_shapes=[pltpu.VMEM((r, 1), _F32), pltpu.VMEM((r, 1), _F32), pltpu.VMEM((r, dh), _F32),
                        pltpu.VMEM((1, ps * h), _F32)],
    )
    kc = cache_k.reshape(n_phys, n_fox, ps * h, dh)
    vc = cache_v.reshape(n_phys, n_fox, ps * h, dh)
    out = pl.pallas_call(
        functools.partial(_decode_kernel, n_heads=h, n_q=nq, dh=dh, n_grp=n_grp, n_steps=n_steps,
                          n_pages=n_pages, n_layers=n_fox, layer=layer),
        grid_spec=grid_spec,
        out_shape=jax.ShapeDtypeStruct((b, nq, h * dh), _F32),
        compiler_params=_params("arbitrary", "arbitrary"),
        name="fox_decode",
    )(page_table, qs, cq, c_new.reshape(b, 1, nq * h), k_new.reshape(b, nq * h, dh).astype(_BF16),
      v_new.reshape(b, nq * h, dh).astype(_BF16), mask, *([kc] * n_grp), *([vc] * n_grp),
      *([sfx] * n_grp), *([tot] * n_grp))
    return out.reshape(b * nq, h * dh).astype(_BF16)


def _first(accs, emn, en):
    return [accs[0]]


def _rwkv_layer(x, shift_prev, wkv_prev, v_first, j, i, W, Wb):
    b, l, d = x.shape
    h, n = wkv_prev.shape[1], wkv_prev.shape[2]
    xr, xw, xk, xv, xa, xg = _rwkv_mix(x, shift_prev, W['norm_mix'][i], W['rwkv_mu'][j])
    (r,) = _mm([xr], [Wb['rwkv_w_r'][j]], epilogue=_first, out_dtypes=[_F32], name="rwkv_r")
    decay = _lora(xw, W['rwkv_w1'][j], W['rwkv_w2'][j], act=jnp.tanh, extras_n=[W['rwkv_w0'][j]],
                  epilogue=lambda acc, en: jnp.exp(-jnp.exp(-_softplus(-(en[0] + acc)) - 0.5)), name="rwkv_decay")
    a = _lora(xa, W['rwkv_a1'][j], W['rwkv_a2'][j], act=lambda t: t, extras_n=[W['rwkv_a0'][j]],
              epilogue=lambda acc, en: _sigmoid(en[0] + acc), name="rwkv_a")
    g = _lora(xg, W['rwkv_g1'][j], W['rwkv_g2'][j], act=_sigmoid, epilogue=lambda acc, en: acc, name="rwkv_g")
    if j == 0:
        (v,) = _mm([xv], [Wb['rwkv_w_v'][j]], epilogue=_first, out_dtypes=[_F32], name="rwkv_v")
        v_first = v
    else:
        gate = _lora(xv, W['rwkv_v1'][j - 1], W['rwkv_v2'][j - 1], act=lambda t: t, extras_n=[W['rwkv_v0'][j - 1]],
                     epilogue=lambda acc, en: _sigmoid(en[0] + acc), name="rwkv_vgate")
        (v,) = _mm([xv], [Wb['rwkv_w_v'][j]], extras_mn=[v_first, gate], out_dtypes=[_F32], name="rwkv_v",
                   epilogue=lambda accs, emn, en: [accs[0] + (emn[0] - accs[0]) * emn[1]])
    (k,) = _mm([xk], [Wb['rwkv_w_k'][j]], epilogue=_first, out_dtypes=[_F32], name="rwkv_k")
    p = -(-(b * h) // _LANES) * _LANES
    seqs = [_to_lanes(t, b, l, h, n, p) for t in (r, decay, k, v, a)]
    params = [_param_lanes(w.reshape(-1), b, h, n, p) for w in
              (W['rwkv_k_k'][j], W['rwkv_k_a'][j], W['rwkv_r_k'][j], W['rwkv_gn_w'][j], W['rwkv_gn_b'][j])]
    s0 = jnp.pad(wkv_prev.astype(_F32).transpose(3, 2, 0, 1).reshape(n, n, b * h), ((0, 0), (0, 0), (0, p - b * h)))
    o, s_last = _rwkv_scan(seqs, params, s0, n)
    og = _from_lanes(o, g, b, l, h, n)
    s_last = s_last[:, :, :b * h].reshape(n, n, b, h).transpose(2, 3, 1, 0)
    x2 = x.reshape(b * l, d)
    (x2,) = _mm([og], [Wb['rwkv_w_o'][j]], extras_mn=[x2], out_dtypes=[_F32], name="rwkv_out",
                epilogue=lambda accs, emn, en: [emn[0] + accs[0]])
    shift_out = _rmsnorm(x[:, -1, :], W['norm_mix'][i], _F32)
    return x2.reshape(b, l, d), shift_out, s_last, v_first


def _fox_layer(x, j, i, W, Wb, paged, kv_bufs):
    b, l, d = x.shape
    h = W['fox_b_f'].shape[1]
    dh = d // h
    x2 = x.reshape(b * l, d)
    hn = _rmsnorm(x2, W['norm_mix'][i], _BF16)
    w_in = Wb['fox_in'][j]
    w_q, w_k, w_v = ((c * d, d) for c in range(3))
    w_f = (3 * d, -(-h // _LANES) * _LANES)
    if paged is None:
        q_scale = dh ** -0.5 * _LOG2E
        (q,) = _mm([hn], [w_in], window=w_q, epilogue=lambda accs, emn, en: [accs[0] * q_scale],
                   out_dtypes=[_BF16], name="fox_q")
    else:
        (q,) = _mm([hn], [w_in], window=w_q, epilogue=_first, out_dtypes=[_F32], name="fox_q")
    twice = lambda accs, emn, en: [accs[0], accs[0]]
    stack_k, stack_v = (None, None) if kv_bufs is None else ((kv_bufs[0], j, l), (kv_bufs[1], j, l))
    k, kb = _mm([hn], [w_in], window=w_k, epilogue=twice, out_dtypes=[_F32, _BF16], name="fox_k", stack=stack_k)
    v, vb = _mm([hn], [w_in], window=w_v, epilogue=twice, out_dtypes=[_F32, _BF16], name="fox_v", stack=stack_v)
    bias_f = jnp.pad(W['fox_b_f'][j], (0, w_f[1] - h))
    (logf_pad,) = _mm([hn], [w_in], window=w_f, extras_n=[bias_f], out_dtypes=[_F32], name="fox_logf",
                      epilogue=lambda accs, emn, en: [-_softplus(-(accs[0] + en[0]))])
    logf = logf_pad[:, :h].reshape(b, l, h)
    c_rows = _cumsum_rows(logf.transpose(0, 2, 1).reshape(b * h, l), _LOG2E if paged is None else 1.0)
    if paged is None:
        o = _fox_attention(q.reshape(b, l, d), kb.reshape(b, l, d), vb.reshape(b, l, d),
                           c_rows.reshape(b * h, 1, l), b, l, h, dh)
        o = o.reshape(b * l, d)
    else:
        cache_k, cache_v, sfx, tot, page_table = paged
        c_new = c_rows.reshape(b, h, l).transpose(0, 2, 1)
        o = _fox_decode(q.reshape(b, l, h, dh), k.reshape(b, l, h, dh), v.reshape(b, l, h, dh), c_new,
                        cache_k, cache_v, sfx, tot, page_table, j)
    (x2,) = _mm([o], [Wb['fox_w_o'][j]], extras_mn=[x2], out_dtypes=[_F32], name="fox_out",
                epilogue=lambda accs, emn, en: [emn[0] + accs[0]])
    return x2.reshape(b, l, d), k, v, logf


def _ffn_ple(x, p_i, i, W, Wb):
    b, l, d = x.shape
    x2 = x.reshape(b * l, d)
    hn = _rmsnorm(x2, W['norm_ffn'][i], _BF16)
    (up,) = _mm([hn], [Wb['w_up'][i]], out_dtypes=[_BF16], name="ffn_up",
                epilogue=lambda accs, emn, en: [jnp.square(jnp.maximum(accs[0], 0.0))])
    (x2,) = _mm([up], [Wb['w_down'][i]], extras_mn=[x2], out_dtypes=[_F32], name="ffn_down",
                epilogue=lambda accs, emn, en: [emn[0] + accs[0]])
    hp = _rmsnorm(x2, W['norm_ple'][i], _BF16)
    pb = p_i.reshape(b * l, -1).astype(_BF16)
    (x2,) = _mm([hp, pb], [Wb['w_ple_gate'][i], Wb['w_ple'][i]], extras_mn=[x2], out_dtypes=[_F32], name="ple",
                epilogue=lambda accs, emn, en: [emn[0] + _sigmoid(accs[0]) * accs[1]])
    return x2.reshape(b, l, d)


def _trunk(x, p, shift_in, wkv_in, paged, W, Wb):
    depth = W['norm_mix'].shape[0]
    b, l, d = x.shape
    n_fox = depth // 2
    h = W['fox_b_f'].shape[1]
    kv_bufs = None
    if paged is None and l % _LANES == 0:
        rows = b * n_fox * l * (d // _LANES)
        kv_bufs = [jnp.zeros((rows, _LANES), _F32), jnp.zeros((rows, _LANES), _F32)]
    v_first = None
    shifts, wkvs, ks, vs, lfs = [], [], [], [], []
    for i in range(depth):
        j = i // 2
        if i % 2 == 0:
            x, sh, s_last, v_first = _rwkv_layer(x, shift_in[j], wkv_in[j], v_first, j, i, W, Wb)
            shifts.append(sh)
            wkvs.append(s_last)
        else:
            x, k, v, lf = _fox_layer(x, j, i, W, Wb, paged, kv_bufs)
            if kv_bufs is not None:
                kv_bufs = [k, v]
            else:
                ks.append(k.reshape(b, l, h, d // h))
                vs.append(v.reshape(b, l, h, d // h))
            lfs.append(lf)
        x = _ffn_ple(x, p[i], i, W, Wb)
    y = _rmsnorm(x.reshape(b * l, d), W['norm_out'], _F32).reshape(b, l, d)
    if kv_bufs is None:
        k_all, v_all = jnp.stack(ks, 1), jnp.stack(vs, 1)
    else:
        k_all, v_all = (t.reshape(b, n_fox, l, h, d // h) for t in kv_bufs)
    return y, jnp.stack(shifts, 0), jnp.stack(wkvs, 0), k_all, v_all, jnp.stack(lfs, 1)


def kernel(x_prompt, x_sample, cache_k, cache_v, cache_logf, state_wkv, state_shift, page_table, p_prompt, p_sample, norm_mix, norm_ffn, norm_ple, norm_out, rwkv_mu, rwkv_w_r, rwkv_w_k, rwkv_w_v, rwkv_w_o, rwkv_w0, rwkv_w1, rwkv_w2, rwkv_a0, rwkv_a1, rwkv_a2, rwkv_v0, rwkv_v1, rwkv_v2, rwkv_g1, rwkv_g2, rwkv_k_k, rwkv_k_a, rwkv_r_k, rwkv_gn_w, rwkv_gn_b, fox_w_in, fox_b_f, fox_w_o, w_up, w_down, w_ple, w_ple_gate):
    W = {
        'norm_mix': norm_mix, 'norm_ffn': norm_ffn, 'norm_ple': norm_ple, 'norm_out': norm_out,
        'rwkv_mu': rwkv_mu, 'rwkv_w0': rwkv_w0, 'rwkv_w1': rwkv_w1, 'rwkv_w2': rwkv_w2,
        'rwkv_a0': rwkv_a0, 'rwkv_a1': rwkv_a1, 'rwkv_a2': rwkv_a2,
        'rwkv_v0': rwkv_v0, 'rwkv_v1': rwkv_v1, 'rwkv_v2': rwkv_v2,
        'rwkv_g1': rwkv_g1, 'rwkv_g2': rwkv_g2, 'rwkv_k_k': rwkv_k_k, 'rwkv_k_a': rwkv_k_a,
        'rwkv_r_k': rwkv_r_k, 'rwkv_gn_w': rwkv_gn_w, 'rwkv_gn_b': rwkv_gn_b, 'fox_b_f': fox_b_f,
    }
    d = x_prompt.shape[-1]
    bf = lambda w: w.astype(_BF16)
    Wb = {
        'rwkv_w_r': bf(rwkv_w_r), 'rwkv_w_k': bf(rwkv_w_k), 'rwkv_w_v': bf(rwkv_w_v), 'rwkv_w_o': bf(rwkv_w_o),
        'fox_in': bf(fox_w_in),
        'fox_w_o': bf(fox_w_o), 'w_up': bf(w_up), 'w_down': bf(w_down),
        'w_ple': bf(w_ple), 'w_ple_gate': bf(w_ple_gate),
    }
    n_rwkv, _, n_heads, n_dim, _ = state_wkv.shape
    b = x_prompt.shape[0]
    shift0 = jnp.zeros((n_rwkv, b, d), x_prompt.dtype)
    wkv0 = jnp.zeros((n_rwkv, b, n_heads, n_dim, n_dim), _F32)
    y_p, shift_p, wkv_p, k_p, v_p, lf_p = _trunk(x_prompt, p_prompt, shift0, wkv0, None, W, Wb)
    sfx, tot = _page_suffix(cache_logf)
    y_s, shift_s, wkv_s, k_s, v_s, lf_s = _trunk(
        x_sample, p_sample, state_shift, state_wkv, (cache_k, cache_v, sfx, tot, page_table), W, Wb)
    return (y_p, y_s, k_p, v_p, lf_p, k_s, v_s, lf_s, wkv_p, shift_p, wkv_s, shift_s)
```
